```python
import math
import jax, jax.numpy as jnp
from jax import lax
import numpy as np

D_MODEL = 1024
BATCH = 8
SEQ = 4096
DEPTH = 4

N_MEM = 256
HEAD_DIM = 64
MIX_WIDTH = 3 * D_MODEL // 4
MIX_HEADS = MIX_WIDTH // HEAD_DIM
MEM_HEADS = 4
MEM_WIDTH = MEM_HEADS * HEAD_DIM
NSA_KV_GROUPS = 4
NSA_GROUP_SIZE = MIX_HEADS // NSA_KV_GROUPS
KV_WIDTH = NSA_KV_GROUPS * HEAD_DIM
CMP_BLOCK = 32
CMP_STRIDE = 16
CMP_HIDDEN = 2 * HEAD_DIM
SEL_BLOCK = 64
SEL_TOPN = 16
WINDOW = 512
NSA_Q_BLOCK = 64
FORCE_SCORE = 1.0e4
NSA_IN = MIX_WIDTH + 6 * KV_WIDTH + 3 * MIX_HEADS + MEM_WIDTH
DECAY_LORA = 64
ICLR_LORA = 64
GATE_LORA = 160
RW_SHIFT = 3 * MIX_WIDTH + DECAY_LORA + ICLR_LORA + GATE_LORA
RW_IN = RW_SHIFT + MEM_WIDTH
GN_EPS = 64e-5
FFN_HIDDEN = -(-8 * D_MODEL // (3 * 256)) * 256
N_NSA = (DEPTH + 1) // 2
N_RWKV = DEPTH // 2
RMS_EPS = 1e-6
NEG_INF = -1e30

kernel_name = "nsa_rwkv7_memxattn_hybrid_trunk"


def _rmsnorm(x, g):
    xf = x.astype(jnp.float32)
    y = xf * lax.rsqrt(jnp.mean(xf * xf, axis=-1, keepdims=True) + RMS_EPS)
    return (y * g.astype(jnp.float32)).astype(x.dtype)


def _split(x, sizes):
    idx = [int(i) for i in np.cumsum(sizes)[:-1]]
    return jnp.split(x, idx, axis=-1)


def _alibi_slopes(n):
    def pow2(m):
        start = 2.0 ** (-8.0 / m)
        return [start ** (i + 1) for i in range(m)]
    c = 2 ** int(math.floor(math.log2(n)))
    s = pow2(c)
    if c < n:
        s = s + pow2(2 * c)[0::2][: n - c]
    return np.asarray(s, dtype=np.float32)


def _masked_softmax(s, mask):
    p = jax.nn.softmax(jnp.where(mask, s, NEG_INF), axis=-1)
    return p * mask


def _compress(kv, pos, w1, w2):
    S = kv.shape[1]
    n_cmp = (S - CMP_BLOCK) // CMP_STRIDE + 1
    idx = jnp.arange(n_cmp)[:, None] * CMP_STRIDE + jnp.arange(CMP_BLOCK)[None, :]
    blocks = kv[:, idx] + pos[None, None, :, None, :]
    hid = jax.nn.gelu(jnp.einsum('bnlgd,lde->bnge', blocks, w1))
    return jnp.einsum('bnge,ed->bngd', hid, w2)


def _nsa_mixer(h, w_in, gate_b, cmp_pos, cmp_w1, cmp_w2):
    B, S, _ = h.shape
    G, R, Dh, QB = NSA_KV_GROUPS, NSA_GROUP_SIZE, HEAD_DIM, NSA_Q_BLOCK
    proj = h @ w_in
    q, kc, vc, ks, vs, kw, vw, gl, q_mem = _split(
        proj, [MIX_WIDTH] + [KV_WIDTH] * 6 + [3 * MIX_HEADS, MEM_WIDTH])
    q = q.reshape(B, S, G, R, Dh)
    kc, vc, ks, vs, kw, vw = (t.reshape(B, S, G, Dh) for t in (kc, vc, ks, vs, kw, vw))
    gates = jax.nn.sigmoid(gl + gate_b).reshape(B, S, 3, G, R)

    kc_b = _compress(kc, cmp_pos[0], cmp_w1[0], cmp_w2[0])
    vc_b = _compress(vc, cmp_pos[1], cmp_w1[1], cmp_w2[1])
    n_cmp = kc_b.shape[1]
    n_blk = S // SEL_BLOCK
    n_sel = min(SEL_TOPN, n_blk)
    ks_blk = ks.reshape(B, n_blk, SEL_BLOCK, G, Dh).transpose(0, 3, 1, 2, 4)
    vs_blk = vs.reshape(B, n_blk, SEL_BLOCK, G, Dh).transpose(0, 3, 1, 2, 4)
    kw_pad = jnp.pad(kw, ((0, 0), (WINDOW, 0), (0, 0), (0, 0)))
    vw_pad = jnp.pad(vw, ((0, 0), (WINDOW, 0), (0, 0), (0, 0)))

    slopes = jnp.asarray(_alibi_slopes(MIX_HEADS).reshape(G, R))[None, :, :, None, None]
    cmp_start = jnp.arange(n_cmp) * CMP_STRIDE
    cmp_end = cmp_start + CMP_BLOCK - 1
    sel_start = jnp.arange(n_blk) * SEL_BLOCK
    overlap = ((cmp_start[:, None] <= sel_start[None, :] + SEL_BLOCK - 1)
               & (cmp_end[:, None] >= sel_start[None, :])).astype(jnp.float32)
    blk_ids = jnp.arange(n_blk)
    b_idx = jnp.arange(B)[:, None, None, None]
    g_idx = jnp.arange(G)[None, :, None, None]
    scale = HEAD_DIM ** -0.5

    def chunk(c0):
        t = c0 + jnp.arange(QB)
        qc = lax.dynamic_slice_in_dim(q, c0, QB, axis=1)
        dist_c = t[:, None] - cmp_end[None, :]
        s_c = jnp.einsum('bqgrd,bngd->bgrqn', qc, kc_b).astype(jnp.float32) * scale \
            - slopes * dist_c.astype(jnp.float32)
        p_c = _masked_softmax(s_c, dist_c >= 0)
        o_c = jnp.einsum('bgrqn,bngd->bqgrd', p_c.astype(vc_b.dtype), vc_b)
        imp = jnp.einsum('bgrqn,nj->bgqj', p_c, overlap)
        tb = (t // SEL_BLOCK)[:, None]
        valid = blk_ids[None, :] <= tb
        forced = (blk_ids[None, :] == 0) | (blk_ids[None, :] == tb) | (blk_ids[None, :] == tb - 1)
        score = jnp.where(valid, jnp.where(forced, FORCE_SCORE, imp), -jnp.inf)
        top_s, top_i = lax.top_k(score, n_sel)
        kg = ks_blk[b_idx, g_idx, top_i].reshape(B, G, QB, n_sel * SEL_BLOCK, Dh)
        vg = vs_blk[b_idx, g_idx, top_i].reshape(B, G, QB, n_sel * SEL_BLOCK, Dh)
        pos = (top_i[..., None] * SEL_BLOCK + jnp.arange(SEL_BLOCK)).reshape(B, G, QB, -1)
        ok = jnp.broadcast_to(jnp.isfinite(top_s)[..., None],
                              (B, G, QB, n_sel, SEL_BLOCK)).reshape(B, G, QB, -1)
        dist_s = t[None, None, :, None] - pos
        mask_s = ok & (dist_s >= 0)
        s_s = jnp.einsum('bqgrd,bgqkd->bgrqk', qc, kg).astype(jnp.float32) * scale \
            - slopes * dist_s[:, :, None].astype(jnp.float32)
        p_s = _masked_softmax(s_s, mask_s[:, :, None])
        o_s = jnp.einsum('bgrqk,bgqkd->bqgrd', p_s.astype(vg.dtype), vg)
        kwc = lax.dynamic_slice_in_dim(kw_pad, c0, WINDOW + QB, axis=1)
        vwc = lax.dynamic_slice_in_dim(vw_pad, c0, WINDOW + QB, axis=1)
        kp = c0 - WINDOW + jnp.arange(WINDOW + QB)
        dist_w = t[:, None] - kp[None, :]
        mask_w = (kp[None, :] >= 0) & (dist_w >= 0) & (dist_w < WINDOW)
        s_w = jnp.einsum('bqgrd,bkgd->bgrqk', qc, kwc).astype(jnp.float32) * scale \
            - slopes * dist_w.astype(jnp.float32)
        p_w = _masked_softmax(s_w, mask_w)
        o_w = jnp.einsum('bgrqk,bkgd->bqgrd', p_w.astype(vwc.dtype), vwc)
        gc = lax.dynamic_slice_in_dim(gates, c0, QB, axis=1)[..., None]
        return gc[:, :, 0] * o_c + gc[:, :, 1] * o_s + gc[:, :, 2] * o_w

    starts = jnp.arange(S // QB) * QB
    o = lax.map(chunk, starts)
    o = jnp.moveaxis(o, 0, 1).reshape(B, S, MIX_WIDTH)
    return o, q_mem


def _wkv7_scan(r, w, k, v, a, b):
    B, S, H, N = r.shape

    def step(state, inp):
        r_t, w_t, k_t, v_t, a_t, b_t = inp
        sa = jnp.einsum('bhvk,bhk->bhv', state, a_t)
        state = state * w_t[:, :, None, :] + sa[..., None] * b_t[:, :, None, :] \
            + v_t[..., None] * k_t[:, :, None, :]
        return state, jnp.einsum('bhvk,bhk->bhv', state, r_t)

    xs = tuple(jnp.moveaxis(t, 1, 0) for t in (r, w, k, v, a, b))
    s0 = jnp.zeros((B, H, N, N), jnp.float32)
    _, y = lax.scan(step, s0, xs)
    return jnp.moveaxis(y, 0, 1)


def _rwkv7_mixer(h, w_in, mu, w0, w2, a0, a2, g2, k_k, k_a, r_k, lnx_w, lnx_b):
    B, S, _ = h.shape
    H, N = MIX_HEADS, HEAD_DIM
    proj = h @ w_in
    z, q_mem = proj[..., :RW_SHIFT], proj[..., RW_SHIFT:]
    z_prev = jnp.pad(z, ((0, 0), (1, 0), (0, 0)))[:, :-1]
    z = z + (z_prev - z) * mu
    r, k, v, zw, za, zg = _split(z, [MIX_WIDTH] * 3 + [DECAY_LORA, ICLR_LORA, GATE_LORA])
    w_log = -jax.nn.softplus(-(w0 + jnp.tanh(zw) @ w2)) - 0.5
    decay = jnp.exp(-jnp.exp(w_log.astype(jnp.float32)))
    a = jax.nn.sigmoid(a0 + za @ a2)
    g = jax.nn.sigmoid(zg) @ g2
    kk = (k * k_k).astype(jnp.float32).reshape(B, S, H, N)
    kk = kk / jnp.maximum(jnp.sqrt(jnp.sum(kk * kk, axis=-1, keepdims=True)), 1e-12)
    k = k * (1 + (a - 1) * k_a)

    def heads(t):
        return t.astype(jnp.float32).reshape(B, S, H, N)

    rh, wh, kh, vh, ah = heads(r), heads(decay), heads(k), heads(v), heads(a)
    y = _wkv7_scan(rh, wh, kh, vh, -kk, kk * ah)
    mean = jnp.mean(y, axis=-1, keepdims=True)
    var = jnp.mean(jnp.square(y - mean), axis=-1, keepdims=True)
    y = ((y - mean) * lax.rsqrt(var + GN_EPS)).reshape(B, S, MIX_WIDTH)
    y = y * lnx_w.astype(jnp.float32) + lnx_b.astype(jnp.float32)
    bonus = jnp.sum(rh * kh * r_k.astype(jnp.float32), axis=-1, keepdims=True) * vh
    out = (y + bonus.reshape(B, S, MIX_WIDTH)) * g.astype(jnp.float32)
    return out.astype(h.dtype), q_mem


def _mem_attention(q_mem, mem_n, w_kv):
    B, S, _ = q_mem.shape
    M = mem_n.shape[1]
    k, v = jnp.split(mem_n @ w_kv, 2, axis=-1)
    q = q_mem.reshape(B, S, MEM_HEADS, HEAD_DIM)
    k = k.reshape(B, M, MEM_HEADS, HEAD_DIM)
    v = v.reshape(B, M, MEM_HEADS, HEAD_DIM)
    s = jnp.einsum('bshd,bmhd->bhsm', q, k).astype(jnp.float32) * (HEAD_DIM ** -0.5)
    p = jax.nn.softmax(s, axis=-1).astype(v.dtype)
    return jnp.einsum('bhsm,bmhd->bshd', p, v).reshape(B, S, MEM_WIDTH)


def _swiglu(h, w_in, w_out):
    gate, up = jnp.split(h @ w_in, 2, axis=-1)
    return (jax.nn.silu(gate) * up) @ w_out


def setup_inputs(seed: int = 0) -> dict:
    key = jax.random.key(seed)
    ks = jax.random.split(key, 32)
    f32 = jnp.float32

    def nrm(k, shape, scale):
        return jax.random.normal(k, shape, f32) * scale

    D = D_MODEL
    return {
        'x': nrm(ks[0], (BATCH, SEQ, D), 1.0),
        'mem': nrm(ks[1], (BATCH, N_MEM, D), 1.0),
        'norm1': 1.0 + nrm(ks[2], (DEPTH, D), 0.05),
        'norm_mem': 1.0 + nrm(ks[3], (DEPTH, D), 0.05),
        'w_mem_kv': nrm(ks[4], (DEPTH, D, 2 * MEM_WIDTH), D ** -0.5),
        'w_o': nrm(ks[5], (DEPTH, MIX_WIDTH + MEM_WIDTH, D), (MIX_WIDTH + MEM_WIDTH) ** -0.5),
        'norm2': 1.0 + nrm(ks[6], (DEPTH, D), 0.05),
        'w_ffn_in': nrm(ks[7], (DEPTH, D, 2 * FFN_HIDDEN), D ** -0.5),
        'w_ffn_out': nrm(ks[8], (DEPTH, FFN_HIDDEN, D), FFN_HIDDEN ** -0.5),
        'nsa_w_in': nrm(ks[9], (N_NSA, D, NSA_IN), D ** -0.5),
        'nsa_gate_b': nrm(ks[10], (N_NSA, 3 * MIX_HEADS), 0.1),
        'nsa_cmp_pos': nrm(ks[11], (N_NSA, 2, CMP_BLOCK, HEAD_DIM), 0.1),
        'nsa_cmp_w1': nrm(ks[12], (N_NSA, 2, CMP_BLOCK, HEAD_DIM, CMP_HIDDEN), (CMP_BLOCK * HEAD_DIM) ** -0.5),
        'nsa_cmp_w2': nrm(ks[13], (N_NSA, 2, CMP_HIDDEN, HEAD_DIM), CMP_HIDDEN ** -0.5),
        'rw_w_in': nrm(ks[14], (N_RWKV, D, RW_IN), D ** -0.5),
        'rw_mu': jax.random.uniform(ks[15], (N_RWKV, RW_SHIFT), f32),
        'rw_w0': jax.random.uniform(ks[16], (N_RWKV, MIX_WIDTH), f32, -5.0, 1.0),
        'rw_w2': nrm(ks[17], (N_RWKV, DECAY_LORA, MIX_WIDTH), 0.1 * DECAY_LORA ** -0.5),
        'rw_a0': nrm(ks[18], (N_RWKV, MIX_WIDTH), 0.5),
        'rw_a2': nrm(ks[19], (N_RWKV, ICLR_LORA, MIX_WIDTH), 0.3 * ICLR_LORA ** -0.5),
        'rw_g2': nrm(ks[20], (N_RWKV, GATE_LORA, MIX_WIDTH), GATE_LORA ** -0.5),
        'rw_k_k': 0.85 + nrm(ks[21], (N_RWKV, MIX_WIDTH), 0.05),
        'rw_k_a': 1.0 + nrm(ks[22], (N_RWKV, MIX_WIDTH), 0.05),
        'rw_r_k': nrm(ks[23], (N_RWKV, MIX_HEADS, HEAD_DIM), 0.1),
        'rw_lnx_w': 1.0 + nrm(ks[24], (N_RWKV, MIX_WIDTH), 0.05),
        'rw_lnx_b': nrm(ks[25], (N_RWKV, MIX_WIDTH), 0.02),
        'final_norm': 1.0 + nrm(ks[26], (D,), 0.05),
    }


def reference(x, mem, norm1, norm_mem, w_mem_kv, w_o, norm2, w_ffn_in, w_ffn_out,
              nsa_w_in, nsa_gate_b, nsa_cmp_pos, nsa_cmp_w1, nsa_cmp_w2,
              rw_w_in, rw_mu, rw_w0, rw_w2, rw_a0, rw_a2, rw_g2, rw_k_k, rw_k_a, rw_r_k,
              rw_lnx_w, rw_lnx_b, final_norm):
    for i in range(DEPTH):
        j = i // 2
        hn = _rmsnorm(x, norm1[i])
        mn = _rmsnorm(mem, norm_mem[i])
        if i % 2 == 0:
            mix, q_mem = _nsa_mixer(hn, nsa_w_in[j], nsa_gate_b[j], nsa_cmp_pos[j],
                                    nsa_cmp_w1[j], nsa_cmp_w2[j])
        else:
            mix, q_mem = _rwkv7_mixer(hn, rw_w_in[j], rw_mu[j], rw_w0[j], rw_w2[j], rw_a0[j],
                                      rw_a2[j], rw_g2[j], rw_k_k[j], rw_k_a[j], rw_r_k[j],
                                      rw_lnx_w[j], rw_lnx_b[j])
        cross = _mem_attention(q_mem, mn, w_mem_kv[i])
        x = x + jnp.concatenate([mix, cross], axis=-1) @ w_o[i]
        x = x + _swiglu(_rmsnorm(x, norm2[i]), w_ffn_in[i], w_ffn_out[i])
    return _rmsnorm(x, final_norm)
```

```python
import functools
import math

import numpy as np
import jax
import jax.numpy as jnp
from jax import lax
from jax.experimental import pallas as pl
from jax.experimental.pallas import tpu as pltpu

F32 = jnp.float32
BF16 = jnp.bfloat16
HI = lax.Precision.HIGHEST

D_MODEL = 1024
HEAD_DIM = 64
MIX_WIDTH = 768
MIX_HEADS = 12
MEM_HEADS = 4
MEM_WIDTH = 256
KV_GROUPS = 4
GROUP_SIZE = 3
KV_WIDTH = 256
CMP_BLOCK = 32
CMP_STRIDE = 16
CMP_HIDDEN = 128
SEL_BLOCK = 64
SEL_TOPN = 16
WINDOW = 512
FORCE_SCORE = 1.0e4
DECAY_LORA = 64
ICLR_LORA = 64
GATE_LORA = 160
GN_EPS = 64e-5
FFN_HIDDEN = 2816
RMS_EPS = 1e-6
NEG_INF = -1e30

LANES = 128
VMEM_LIMIT = 56 * 1024 * 1024

_NT = (((1,), (1,)), ((), ()))
_TN = (((0,), (0,)), ((), ()))


def _mm(a, b, dims=None, hi=False):
    if hi:
        a, b, prec = a.astype(F32), b.astype(F32), HI
    else:
        a, b, prec = a.astype(BF16), b.astype(BF16), None
    if dims is None:
        return jnp.dot(a, b, preferred_element_type=F32, precision=prec)
    return lax.dot_general(a, b, dims, preferred_element_type=F32, precision=prec)


def _iota(shape, dim):
    return lax.broadcasted_iota(jnp.int32, shape, dim)


def _rms(x, g):
    ms = jnp.mean(x * x, axis=-1, keepdims=True)
    return x * lax.rsqrt(ms + RMS_EPS) * g


def _cparams(sem):
    return pltpu.CompilerParams(dimension_semantics=sem, vmem_limit_bytes=VMEM_LIMIT)


def _mem_kv_kernel(mem_ref, g_ref, wkt_ref, wv_ref, kt_ref, v_ref):
    mn = _rms(mem_ref[0], g_ref[0]).astype(BF16)
    kt = _mm(wkt_ref[0], mn, _NT)
    v = _mm(mn, wv_ref[0])
    rowh = _iota(kt.shape, 0) // HEAD_DIM
    colh = _iota(v.shape, 1) // HEAD_DIM
    for h in range(MEM_HEADS):
        kt_ref[0, 0, h] = jnp.where(rowh == h, kt, 0.0).astype(BF16)
        v_ref[0, 0, h] = jnp.where(colh == h, v, 0.0).astype(BF16)


def _mem_kv(mem, norm_mem, w_mem_kv):
    B, M, D = mem.shape
    L = norm_mem.shape[0]
    wkt = jnp.swapaxes(w_mem_kv[:, :, :MEM_WIDTH], 1, 2).astype(BF16)
    wv = w_mem_kv[:, :, MEM_WIDTH:].astype(BF16)
    return pl.pallas_call(
        _mem_kv_kernel,
        grid=(L, B),
        in_specs=[
            pl.BlockSpec((1, M, D), lambda l, b: (b, 0, 0)),
            pl.BlockSpec((1, 1, D), lambda l, b: (l, 0, 0)),
            pl.BlockSpec((1, MEM_WIDTH, D), lambda l, b: (l, 0, 0)),
            pl.BlockSpec((1, D, MEM_WIDTH), lambda l, b: (l, 0, 0)),
        ],
        out_specs=[
            pl.BlockSpec((1, 1, MEM_HEADS, MEM_WIDTH, M), lambda l, b: (l, b, 0, 0, 0)),
            pl.BlockSpec((1, 1, MEM_HEADS, M, MEM_WIDTH), lambda l, b: (l, b, 0, 0, 0)),
        ],
        out_shape=[
            jax.ShapeDtypeStruct((L, B, MEM_HEADS, MEM_WIDTH, M), BF16),
            jax.ShapeDtypeStruct((L, B, MEM_HEADS, M, MEM_WIDTH), BF16),
        ],
        compiler_params=_cparams(("parallel", "parallel")),
        name="mem_kv",
    )(mem, norm_mem.reshape(L, 1, D), wkt, wv)


def _mem_attn_kernel(q_ref, kt_ref, v_ref, o_ref):
    q = q_ref[0]
    acc = jnp.zeros(q.shape, F32)
    for h in range(MEM_HEADS):
        s = _mm(q, kt_ref[0, h])
        m = jnp.max(s, axis=-1, keepdims=True)
        p = jnp.exp(s - m)
        l = jnp.sum(p, axis=-1, keepdims=True)
        acc = acc + _mm(p * (1.0 / l), v_ref[0, h])
    o_ref[0] = acc.astype(BF16)


def _mem_attn(qm, ktm, vm):
    B, S, _ = qm.shape
    M = ktm.shape[-1]
    TM = min(512, S)
    return pl.pallas_call(
        _mem_attn_kernel,
        grid=(B, S // TM),
        in_specs=[
            pl.BlockSpec((1, TM, MEM_WIDTH), lambda b, s: (b, s, 0)),
            pl.BlockSpec((1, MEM_HEADS, MEM_WIDTH, M), lambda b, s: (b, 0, 0, 0)),
            pl.BlockSpec((1, MEM_HEADS, M, MEM_WIDTH), lambda b, s: (b, 0, 0, 0)),
        ],
        out_specs=pl.BlockSpec((1, TM, MEM_WIDTH), lambda b, s: (b, s, 0)),
        out_shape=jax.ShapeDtypeStruct((B, S, MEM_WIDTH), BF16),
        compiler_params=_cparams(("parallel", "parallel")),
        name="mem_attn",
    )(qm, ktm, vm)


def _post_kernel(x_ref, mix_ref, cross_ref, wom_ref, woc_ref, g2_ref, wg_ref, wu_ref, wout_ref,
                 gf_ref, o_ref, x1_scr, hn_scr, acc_scr, *, n_parts, n_h, final):
    h = pl.program_id(2)

    @pl.when(h == 0)
    def _():
        x1 = x_ref[0] + _mm(cross_ref[0], woc_ref[...])
        for p in range(n_parts):
            x1 = x1 + _mm(mix_ref[0, p], wom_ref[p])
        x1_scr[...] = x1
        hn_scr[...] = _rms(x1, g2_ref[...]).astype(BF16)
        acc_scr[...] = jnp.zeros(acc_scr.shape, F32)

    hn = hn_scr[...]
    gate = _mm(hn, wg_ref[...])
    up = _mm(hn, wu_ref[...])
    hid = gate * jax.nn.sigmoid(gate) * up
    acc_scr[...] += _mm(hid, wout_ref[...])

    @pl.when(h == n_h - 1)
    def _():
        y = x1_scr[...] + acc_scr[...]
        if final:
            y = _rms(y, gf_ref[...])
        o_ref[0] = y


def _post(x, mix, cross, w_o, norm2, w_ffn_in, w_ffn_out, final_norm, final):
    B, S, D = x.shape
    P, W = mix.shape[1], mix.shape[3]
    TM = min(1024, S)
    TH = 256
    NH = FFN_HIDDEN // TH
    wom = w_o[:MIX_WIDTH].reshape(P, W, D).astype(BF16)
    woc = w_o[MIX_WIDTH:].astype(BF16)
    wi = w_ffn_in.astype(BF16)
    wo = w_ffn_out.astype(BF16)
    kern = functools.partial(_post_kernel, n_parts=P, n_h=NH, final=final)
    return pl.pallas_call(
        kern,
        grid=(B, S // TM, NH),
        in_specs=[
            pl.BlockSpec((1, TM, D), lambda b, s, h: (b, s, 0)),
            pl.BlockSpec((1, P, TM, W), lambda b, s, h: (b, 0, s, 0)),
            pl.BlockSpec((1, TM, MEM_WIDTH), lambda b, s, h: (b, s, 0)),
            pl.BlockSpec((P, W, D), lambda b, s, h: (0, 0, 0)),
            pl.BlockSpec((MEM_WIDTH, D), lambda b, s, h: (0, 0)),
            pl.BlockSpec((1, D), lambda b, s, h: (0, 0)),
            pl.BlockSpec((D, TH), lambda b, s, h: (0, h)),
            pl.BlockSpec((D, TH), lambda b, s, h: (0, NH + h)),
            pl.BlockSpec((TH, D), lambda b, s, h: (h, 0)),
            pl.BlockSpec((1, D), lambda b, s, h: (0, 0)),
        ],
        out_specs=pl.BlockSpec((1, TM, D), lambda b, s, h: (b, s, 0)),
        out_shape=jax.ShapeDtypeStruct((B, S, D), F32),
        scratch_shapes=[
            pltpu.VMEM((TM, D), F32),
            pltpu.VMEM((TM, D), BF16),
            pltpu.VMEM((TM, D), F32),
        ],
        compiler_params=_cparams(("parallel", "parallel", "arbitrary")),
        name="post_ffn",
    )(x, mix, cross, wom, woc, norm2.reshape(1, D), wi, wi, wo, final_norm.reshape(1, D))


NSA_ROW_W = MIX_WIDTH + 4 * KV_WIDTH + MEM_WIDTH + LANES


def _nsa_proj_kernel(x_ref, g_ref, w_ref, wt_ref, gb_ref, q_ref, kc_ref, vc_ref, vs_ref, vw_ref,
                     kst_ref, kwt_ref, qm_ref, gate_ref):
    hn = _rms(x_ref[0], g_ref[...]).astype(BF16)
    res = _mm(hn, w_ref[...])
    for h in range(MIX_HEADS):
        q_ref[0, h] = res[:, h * HEAD_DIM:(h + 1) * HEAD_DIM].astype(BF16)
    off = MIX_WIDTH
    for ref in (kc_ref, vc_ref, vs_ref, vw_ref):
        for g in range(KV_GROUPS):
            ref[0, g] = res[:, off + g * HEAD_DIM: off + (g + 1) * HEAD_DIM].astype(BF16)
        off += KV_WIDTH
    qm_ref[0] = res[:, off:off + MEM_WIDTH].astype(BF16)
    off += MEM_WIDTH
    gates = jax.nn.sigmoid(res[:, off:off + 64] + gb_ref[...])
    for g in range(KV_GROUPS):
        gate_ref[0, g] = gates[:, g * 16:(g + 1) * 16]
    rt = _mm(wt_ref[...], hn, _NT)
    kst_ref[0] = rt[:KV_WIDTH].astype(BF16)
    kwt_ref[0] = rt[KV_WIDTH:].astype(BF16)


def _nsa_proj(x, norm1, w_in, gate_b):
    B, S, D = x.shape
    TM = min(512, S)
    G, H, Dh = KV_GROUPS, MIX_HEADS, HEAD_DIM
    scale = HEAD_DIM ** -0.5
    o = np.cumsum([0, MIX_WIDTH] + [KV_WIDTH] * 6 + [3 * MIX_HEADS, MEM_WIDTH])
    wq, wkc, wvc, wks, wvs, wkw, wvw, wgl, wqm = (w_in[:, o[i]:o[i + 1]] for i in range(9))
    wgl = wgl.reshape(D, 3, G, GROUP_SIZE).transpose(0, 2, 1, 3).reshape(D, G, 9)
    wgl = jnp.pad(wgl, ((0, 0), (0, 0), (0, 7))).reshape(D, 64)
    gb = gate_b.reshape(3, G, GROUP_SIZE).transpose(1, 0, 2).reshape(G, 9)
    gb = jnp.pad(gb, ((0, 0), (0, 7))).reshape(1, 64)
    w_row = jnp.concatenate(
        [wq * scale, wkc, wvc, wvs, wvw, wqm * scale, wgl, jnp.zeros((D, LANES - 64), F32)],
        axis=1).astype(BF16)
    w_t = jnp.concatenate([wks, wkw], axis=1).T.astype(BF16)
    head = lambda n: pl.BlockSpec((1, n, TM, Dh), lambda b, s: (b, 0, s, 0))
    return pl.pallas_call(
        _nsa_proj_kernel,
        grid=(B, S // TM),
        in_specs=[
            pl.BlockSpec((1, TM, D), lambda b, s: (b, s, 0)),
            pl.BlockSpec((1, D), lambda b, s: (0, 0)),
            pl.BlockSpec((D, NSA_ROW_W), lambda b, s: (0, 0)),
            pl.BlockSpec((2 * KV_WIDTH, D), lambda b, s: (0, 0)),
            pl.BlockSpec((1, 64), lambda b, s: (0, 0)),
        ],
        out_specs=[
            head(H), head(G), head(G), head(G), head(G),
            pl.BlockSpec((1, KV_WIDTH, TM), lambda b, s: (b, 0, s)),
            pl.BlockSpec((1, KV_WIDTH, TM), lambda b, s: (b, 0, s)),
            pl.BlockSpec((1, TM, MEM_WIDTH), lambda b, s: (b, s, 0)),
            pl.BlockSpec((1, G, TM, 16), lambda b, s: (b, 0, s, 0)),
        ],
        out_shape=[
            jax.ShapeDtypeStruct((B, H, S, Dh), BF16),
            jax.ShapeDtypeStruct((B, G, S, Dh), BF16),
            jax.ShapeDtypeStruct((B, G, S, Dh), BF16),
            jax.ShapeDtypeStruct((B, G, S, Dh), BF16),
            jax.ShapeDtypeStruct((B, G, S, Dh), BF16),
            jax.ShapeDtypeStruct((B, KV_WIDTH, S), BF16),
            jax.ShapeDtypeStruct((B, KV_WIDTH, S), BF16),
            jax.ShapeDtypeStruct((B, S, MEM_WIDTH), BF16),
            jax.ShapeDtypeStruct((B, G, S, 16), F32),
        ],
        compiler_params=_cparams(("parallel", "parallel")),
        name="nsa_proj",
    )(x, norm1.reshape(1, D), w_row, w_t, gb)


def _compress_kernel(kc_ref, vc_ref, w1_ref, w2_ref, pos_ref, kcb_ref, vcb_ref):
    half = CMP_STRIDE * HEAD_DIM
    for idx, (src, dst) in enumerate(((kc_ref, kcb_ref), (vc_ref, vcb_ref))):
        c = src[0, 0]
        ncp = c.shape[0]
        a = _mm(c, w1_ref[idx, :half])
        bm = _mm(c, w1_ref[idx, half:])
        bias = _mm(pos_ref[idx], w1_ref[idx])[0:1]
        hid = jax.nn.gelu(a + pltpu.roll(bm, ncp - 1, 0) + bias)
        out = _mm(hid, w2_ref[idx])
        row = _iota(out.shape, 0)
        dst[0, 0] = jnp.where(row < ncp - 1, out, 0.0).astype(BF16)


def _compress(kc, vc, cmp_pos, cmp_w1, cmp_w2):
    B, G, S, Dh = kc.shape
    NCP = S // CMP_STRIDE
    kcr = kc.reshape(B, G, NCP, CMP_STRIDE * Dh)
    vcr = vc.reshape(B, G, NCP, CMP_STRIDE * Dh)
    w1 = cmp_w1.reshape(2, CMP_BLOCK * Dh, CMP_HIDDEN).astype(BF16)
    w2 = cmp_w2.astype(BF16)
    pos = jnp.broadcast_to(cmp_pos.reshape(2, 1, CMP_BLOCK * Dh), (2, 8, CMP_BLOCK * Dh)).astype(BF16)
    blk = pl.BlockSpec((1, 1, NCP, CMP_STRIDE * Dh), lambda b, g: (b, g, 0, 0))
    oblk = pl.BlockSpec((1, 1, NCP, Dh), lambda b, g: (b, g, 0, 0))
    return pl.pallas_call(
        _compress_kernel,
        grid=(B, G),
        in_specs=[
            blk, blk,
            pl.BlockSpec((2, CMP_BLOCK * Dh, CMP_HIDDEN), lambda b, g: (0, 0, 0)),
            pl.BlockSpec((2, CMP_HIDDEN, Dh), lambda b, g: (0, 0, 0)),
            pl.BlockSpec((2, 8, CMP_BLOCK * Dh), lambda b, g: (0, 0, 0)),
        ],
        out_specs=[oblk, oblk],
        out_shape=[jax.ShapeDtypeStruct((B, G, NCP, Dh), BF16)] * 2,
        compiler_params=_cparams(("parallel", "parallel")),
        name="nsa_compress",
    )(kcr, vcr, w1, w2, pos)


def _alibi_slopes(n):
    def pow2(m):
        start = 2.0 ** (-8.0 / m)
        return [start ** (i + 1) for i in range(m)]
    c = 2 ** int(math.floor(math.log2(n)))
    s = pow2(c)
    if c < n:
        s = s + pow2(2 * c)[0::2][: n - c]
    return np.asarray(s, dtype=np.float32)


def _nsa_attn_kernel(slope_ref, q_ref, kcb_ref, vcb_ref, ovt_ref, kst_ref, vs_ref, kwt_ref, vw_ref,
                     e_ref, gate_ref, o_ref, score_scr, m_scr, l_scr, acc_scr, *, TQ, TK, S, n_sel):
    R, Dh = GROUP_SIZE, HEAD_DIM
    g = pl.program_id(1)
    t0 = pl.program_id(2) * TQ
    NCP, NB = S // CMP_STRIDE, S // SEL_BLOCK
    qs = q_ref[0].reshape(R * TQ, Dh)
    slopes = [slope_ref[R * g + r] for r in range(R)]
    tq_col = t0 + _iota((TQ, 1), 0)

    sc = _mm(qs, kcb_ref[0, 0], _NT)
    n_row = _iota((1, NCP), 1)
    cend = n_row * CMP_STRIDE + (CMP_BLOCK - 1)
    mask_c = (cend <= tq_col) & (n_row < NCP - 1)
    dpos_c = (cend - t0).astype(F32)
    psum = jnp.zeros((TQ, NCP), F32)
    pcs = []
    for r in range(R):
        s = jnp.where(mask_c, sc[r * TQ:(r + 1) * TQ] + slopes[r] * dpos_c, NEG_INF)
        m = jnp.max(s, axis=-1, keepdims=True)
        p = jnp.where(mask_c, jnp.exp(s - m), 0.0)
        l = jnp.sum(p, axis=-1, keepdims=True)
        p = p * jnp.where(l > 0.0, 1.0 / l, 0.0)
        psum = psum + p
        pcs.append(p)
    o_c = _mm(jnp.concatenate(pcs, axis=0), vcb_ref[0, 0])

    imp_t = _mm(ovt_ref[...], psum, _NT, hi=True)
    tb = lax.shift_right_logical(t0 + _iota((1, TQ), 1), 6)
    j_col = _iota((NB, 1), 0)
    valid = j_col <= tb
    forced = (j_col == 0) | (j_col == tb) | (j_col == tb - 1)
    score = jnp.where(valid, jnp.where(forced, FORCE_SCORE, imp_t), -jnp.inf)
    score_scr[...] = score
    n_valid = (t0 + TQ - 1) // SEL_BLOCK + 1

    def rank_body(i, cnt):
        row = score_scr[pl.ds(i, 1), :]
        tie = jnp.where(i < j_col, 1.0, 0.0)
        return cnt + jnp.where(row > score, 1.0, jnp.where(row == score, tie, 0.0))

    rank = lax.fori_loop(0, n_valid, rank_body, jnp.zeros((NB, TQ), F32))
    sel_t = jnp.where(valid, jnp.where(rank < n_sel, 1.0, 0.0), 0.0)
    sel = sel_t.T.astype(BF16)

    m_scr[...] = jnp.full(m_scr.shape, NEG_INF, F32)
    l_scr[...] = jnp.zeros(l_scr.shape, F32)
    acc_scr[...] = jnp.zeros(acc_scr.shape, F32)
    n_kt = (t0 + TQ - 1) // TK + 1

    def sel_body(kt, carry):
        k0 = pl.multiple_of(kt * TK, TK)
        s_all = _mm(qs, kst_ref[0, :, pl.ds(k0, TK)])
        mf = _mm(sel, e_ref[:, pl.ds(k0, TK)])
        kpos = k0 + _iota((1, TK), 1)
        mask = (mf > 0.5) & (kpos <= tq_col)
        dpos = (kpos - t0).astype(F32)
        v = vs_ref[0, 0, pl.ds(k0, TK), :]
        for r in range(R):
            rows = slice(r * TQ, (r + 1) * TQ)
            s = jnp.where(mask, s_all[rows] + slopes[r] * dpos, NEG_INF)
            m_old = m_scr[rows]
            m_new = jnp.maximum(m_old, jnp.max(s, axis=-1, keepdims=True))
            alpha = jnp.exp(m_old - m_new)
            p = jnp.where(mask, jnp.exp(s - m_new), 0.0)
            l_scr[rows] = alpha * l_scr[rows] + jnp.sum(p, axis=-1, keepdims=True)
            acc_scr[rows] = alpha * acc_scr[rows] + _mm(p, v)
            m_scr[rows] = m_new
        return carry

    lax.fori_loop(0, n_kt, sel_body, 0)

    WK = WINDOW + TQ
    w0 = pl.multiple_of(jnp.maximum(t0 - WINDOW, 0), LANES)
    sw = _mm(qs, kwt_ref[0, :, pl.ds(w0, WK)])
    kpos_w = w0 + _iota((1, WK), 1)
    dist_w = tq_col - kpos_w
    mask_w = (dist_w >= 0) & (dist_w < WINDOW)
    dpos_w = (kpos_w - t0).astype(F32)
    pws = []
    for r in range(R):
        s = jnp.where(mask_w, sw[r * TQ:(r + 1) * TQ] + slopes[r] * dpos_w, NEG_INF)
        m = jnp.max(s, axis=-1, keepdims=True)
        p = jnp.where(mask_w, jnp.exp(s - m), 0.0)
        pws.append(p * (1.0 / jnp.sum(p, axis=-1, keepdims=True)))
    o_w = _mm(jnp.concatenate(pws, axis=0), vw_ref[0, 0, pl.ds(w0, WK), :])

    o_s = acc_scr[...] * (1.0 / l_scr[...])
    gv = gate_ref[0, 0]
    for r in range(R):
        rows = slice(r * TQ, (r + 1) * TQ)
        out = (gv[:, r:r + 1] * o_c[rows] + gv[:, R + r:R + r + 1] * o_s[rows]
               + gv[:, 2 * R + r:2 * R + r + 1] * o_w[rows])
        o_ref[0, 0, :, r * Dh:(r + 1) * Dh] = out.astype(BF16)


def _nsa_attn(q, kcb, vcb, kst, vs, kwt, vw, gates):
    B, H, S, Dh = q.shape
    G, R = KV_GROUPS, GROUP_SIZE
    TQ = 128
    TK = min(512, S)
    NCP, NB = S // CMP_STRIDE, S // SEL_BLOCK
    n_sel = min(SEL_TOPN, NB)
    assert S % TK == 0 and S >= WINDOW + TQ
    slopes = jnp.asarray(_alibi_slopes(MIX_HEADS))
    cs = np.arange(NCP) * CMP_STRIDE
    ss = np.arange(NB) * SEL_BLOCK
    ov = (cs[:, None] <= ss[None, :] + SEL_BLOCK - 1) & (cs[:, None] + CMP_BLOCK - 1 >= ss[None, :])
    ov[NCP - 1] = False
    ovt = jnp.asarray(ov.T.astype(np.float32))
    e = jnp.asarray((np.arange(S)[None, :] // SEL_BLOCK == np.arange(NB)[:, None]).astype(np.float32), BF16)
    kern = functools.partial(_nsa_attn_kernel, TQ=TQ, TK=TK, S=S, n_sel=n_sel)
    return pl.pallas_call(
        kern,
        grid=(B, G, S // TQ),
        in_specs=[
            pl.BlockSpec(memory_space=pltpu.SMEM),
            pl.BlockSpec((1, R, TQ, Dh), lambda b, g, i: (b, g, i, 0)),
            pl.BlockSpec((1, 1, NCP, Dh), lambda b, g, i: (b, g, 0, 0)),
            pl.BlockSpec((1, 1, NCP, Dh), lambda b, g, i: (b, g, 0, 0)),
            pl.BlockSpec((NB, NCP), lambda b, g, i: (0, 0)),
            pl.BlockSpec((1, Dh, S), lambda b, g, i: (b, g, 0)),
            pl.BlockSpec((1, 1, S, Dh), lambda b, g, i: (b, g, 0, 0)),
            pl.BlockSpec((1, Dh, S), lambda b, g, i: (b, g, 0)),
            pl.BlockSpec((1, 1, S, Dh), lambda b, g, i: (b, g, 0, 0)),
            pl.BlockSpec((NB, S), lambda b, g, i: (0, 0)),
            pl.BlockSpec((1, 1, TQ, 16), lambda b, g, i: (b, g, i, 0)),
        ],
        out_specs=pl.BlockSpec((1, 1, TQ, R * Dh), lambda b, g, i: (b, g, i, 0)),
        out_shape=jax.ShapeDtypeStruct((B, G, S, R * Dh), BF16),
        scratch_shapes=[
            pltpu.VMEM((NB, TQ), F32),
            pltpu.VMEM((R * TQ, 1), F32),
            pltpu.VMEM((R * TQ, 1), F32),
            pltpu.VMEM((R * TQ, Dh), F32),
        ],
        compiler_params=_cparams(("parallel", "parallel", "arbitrary")),
        name="nsa_attn",
    )(slopes, q, kcb, vcb, ovt, kst, vs, kwt, vw, e, gates)


RW_Z_W = 3 * MIX_WIDTH + 2 * LANES + 2 * LANES
RW_ROW_W = RW_Z_W + MEM_WIDTH


def _rw_prep_kernel(x_ref, g_ref, w_ref, mu_ref, w0_ref, w2_ref, a0_ref, a2_ref, g2_ref, kk_ref,
                    ka_ref, r_ref, lw_ref, kx_ref, km_ref, v_ref, a_ref, go_ref, qm_ref, carry_scr):
    W = MIX_WIDTH

    @pl.when(pl.program_id(1) == 0)
    def _():
        carry_scr[...] = jnp.zeros(carry_scr.shape, F32)

    hn = _rms(x_ref[0], g_ref[...]).astype(BF16)
    res = _mm(hn, w_ref[...])
    qm_ref[0] = res[:, RW_Z_W:].astype(BF16)
    z = res[:, :RW_Z_W]
    tm = z.shape[0]
    zprev = jnp.where(_iota((tm, 1), 0) == 0, carry_scr[0:1, :], pltpu.roll(z, 1, 0))
    carry_scr[0:1, :] = z[tm - 1:tm, :]
    z = z + (zprev - z) * mu_ref[...]
    r, k, v = z[:, :W], z[:, W:2 * W], z[:, 2 * W:3 * W]
    zw = z[:, 3 * W:3 * W + LANES]
    za = z[:, 3 * W + LANES:3 * W + 2 * LANES]
    zg = z[:, 3 * W + 2 * LANES:]
    w_log = -jax.nn.softplus(-(w0_ref[...] + _mm(jnp.tanh(zw), w2_ref[...]))) - 0.5
    a = jax.nn.sigmoid(a0_ref[...] + _mm(za, a2_ref[...]))
    r_ref[0] = r
    lw_ref[0] = -jnp.exp(w_log)
    kx_ref[0] = k * kk_ref[...]
    km_ref[0] = k * (1.0 + (a - 1.0) * ka_ref[...])
    v_ref[0] = v
    a_ref[0] = a
    go_ref[0] = _mm(jax.nn.sigmoid(zg), g2_ref[...])


def _pad_rows(w, n):
    return jnp.pad(w, ((0, n - w.shape[0]), (0, 0)))


def _rw_prep(x, norm1, w_in, mu, w0, w2, a0, a2, g2, k_k, k_a):
    B, S, D = x.shape
    W = MIX_WIDTH
    TM = min(256, S)
    o = np.cumsum([0, 3 * W, DECAY_LORA, ICLR_LORA, GATE_LORA, MEM_WIDTH])
    seg = [w_in[:, o[i]:o[i + 1]] for i in range(5)]
    padc = lambda w, n: jnp.pad(w, ((0, 0), (0, n - w.shape[1])))
    w_row = jnp.concatenate(
        [seg[0], padc(seg[1], LANES), padc(seg[2], LANES), padc(seg[3], 2 * LANES),
         seg[4] * HEAD_DIM ** -0.5], axis=1).astype(BF16)
    mus = [mu[o[i]:o[i + 1]] for i in range(4)]
    padv = lambda v, n: jnp.pad(v, (0, n - v.shape[0]))
    mu_p = jnp.concatenate([mus[0], padv(mus[1], LANES), padv(mus[2], LANES),
                            padv(mus[3], 2 * LANES)]).reshape(1, RW_Z_W)
    vec = lambda v: v.reshape(1, W)
    full = lambda a: pl.BlockSpec(a.shape, lambda b, s: (0,) * a.ndim)
    args = [norm1.reshape(1, D), w_row, mu_p, vec(w0), _pad_rows(w2, LANES).astype(BF16), vec(a0),
            _pad_rows(a2, LANES).astype(BF16), _pad_rows(g2, 2 * LANES).astype(BF16), vec(k_k), vec(k_a)]
    oblk = pl.BlockSpec((1, TM, W), lambda b, s: (b, s, 0))
    return pl.pallas_call(
        _rw_prep_kernel,
        grid=(B, S // TM),
        in_specs=[pl.BlockSpec((1, TM, D), lambda b, s: (b, s, 0))] + [full(a) for a in args],
        out_specs=[oblk] * 7 + [pl.BlockSpec((1, TM, MEM_WIDTH), lambda b, s: (b, s, 0))],
        out_shape=[jax.ShapeDtypeStruct((B, S, W), F32)] * 7
        + [jax.ShapeDtypeStruct((B, S, MEM_WIDTH), BF16)],
        scratch_shapes=[pltpu.VMEM((8, RW_Z_W), F32)],
        compiler_params=_cparams(("parallel", "arbitrary")),
        name="rw_prep",
    )(x, *args)


def _tri_inverse(a2, eye, same16, same32, same64):
    d = jnp.where(same16, a2, 0.0)
    p = eye + d
    for _ in range(3):
        d = _mm(d, d, hi=True)
        p = p + _mm(p, d, hi=True)
    for lo, hi_ in ((same16, same32), (same32, same64)):
        off = jnp.where(hi_ & jnp.logical_not(lo), a2, 0.0)
        p = p + _mm(_mm(p, off, hi=True), p, hi=True)
    return p


def _rw_scan_kernel(r_ref, lw_ref, kx_ref, km_ref, v_ref, a_ref, g_ref, rk_ref, lnw_ref, lnb_ref,
                    o_ref, st_scr, *, TS, C):
    Dh = HEAD_DIM

    @pl.when(pl.program_id(2) == 0)
    def _():
        st_scr[...] = jnp.zeros(st_scr.shape, F32)

    head0 = _iota((1, LANES), 1) < Dh
    ltri = jnp.where(_iota((C, C), 1) <= _iota((C, C), 0), 1.0, 0.0)
    col2 = _iota((C, LANES), 1) & (Dh - 1)
    row2 = _iota((C, LANES), 0)
    m_incl = col2 <= row2
    m_strict = col2 < row2
    r128 = _iota((LANES, LANES), 0)
    c128 = _iota((LANES, LANES), 1)
    eye = jnp.where(r128 == c128, 1.0, 0.0)
    same16 = (r128 >> 4) == (c128 >> 4)
    same32 = (r128 >> 5) == (c128 >> 5)
    same64 = (r128 >> 6) == (c128 >> 6)
    zeros = jnp.zeros((C, LANES), F32)
    rk, lnw, lnb = rk_ref[...], lnw_ref[...], lnb_ref[...]

    def hsum(x):
        s0 = jnp.sum(jnp.where(head0, x, 0.0), axis=-1, keepdims=True)
        s1 = jnp.sum(jnp.where(head0, 0.0, x), axis=-1, keepdims=True)
        return jnp.where(head0, s0, s1)

    def chunk(c, carry):
        sl = pl.ds(pl.multiple_of(c * C, C), C)
        r, lw, kx, km = r_ref[0, sl, :], lw_ref[0, sl, :], kx_ref[0, sl, :], km_ref[0, sl, :]
        v, a = v_ref[0, sl, :], a_ref[0, sl, :]
        kk = kx / jnp.maximum(jnp.sqrt(hsum(kx * kx)), 1e-12)
        cum = _mm(ltri, lw, hi=True)
        p_in = jnp.exp(cum)
        p_c = p_in[C - 1:C, :]
        at = -kk * jnp.exp(cum - lw)
        rt = r * p_in
        p_inv = jnp.exp(-cum)
        bk = jnp.concatenate([kk * a * p_inv, km * p_inv], axis=0)
        at0, at1 = jnp.where(head0, at, 0.0), jnp.where(head0, 0.0, at)
        rt0, rt1 = jnp.where(head0, rt, 0.0), jnp.where(head0, 0.0, rt)
        aa = _mm(jnp.concatenate([at0, at1, rt0, rt1], axis=0), bk, _NT, hi=True)
        aa0 = jnp.where(m_strict, aa[0:C], 0.0)
        aa1 = pltpu.roll(jnp.where(m_strict, aa[C:2 * C], 0.0), Dh, 1)
        ar0 = jnp.where(m_incl, aa[2 * C:3 * C], 0.0)
        ar1 = jnp.where(m_incl, aa[3 * C:4 * C], 0.0)
        a2 = jnp.concatenate([jnp.where(head0, aa0, 0.0), jnp.where(head0, 0.0, aa1)], axis=0)
        t2 = _tri_inverse(a2, eye, same16, same32, same64)
        x0 = _mm(aa0, jnp.concatenate([zeros, v], axis=0), hi=True)
        x1 = _mm(aa1, jnp.concatenate([v, zeros], axis=0), hi=True)
        w2 = _mm(t2, jnp.concatenate([at0, at1], axis=0), hi=True)
        u2 = _mm(t2, jnp.concatenate([x0, x1], axis=0), hi=True)
        wt = w2[:C] + w2[C:]
        ut = jnp.where(head0, u2[:C], u2[C:])
        st = st_scr[...]
        u = _mm(wt, st, _NT, hi=True) + ut
        uv = jnp.concatenate([u, v], axis=0)
        y = _mm(rt, st, _NT, hi=True) + jnp.where(head0, _mm(ar0, uv, hi=True), _mm(ar1, uv, hi=True))
        st_scr[...] = st * p_c + jnp.where(same64, _mm(uv, bk * p_c, _TN, hi=True), 0.0)
        mean = hsum(y) * (1.0 / Dh)
        d = y - mean
        var = hsum(d * d) * (1.0 / Dh)
        yn = d * lax.rsqrt(var + GN_EPS) * lnw + lnb
        bonus = hsum(r * km * rk) * v
        o_ref[0, sl, :] = ((yn + bonus) * g_ref[0, sl, :]).astype(BF16)
        return carry

    lax.fori_loop(0, TS // C, chunk, 0)


def _rw_scan(r, lw, kx, km, v, a, g, r_k, lnx_w, lnx_b):
    B, S, W = r.shape
    TS = min(512, S)
    C = 64
    blk = pl.BlockSpec((1, TS, LANES), lambda b, p, s: (b, s, p))
    vblk = pl.BlockSpec((1, LANES), lambda b, p, s: (0, p))
    kern = functools.partial(_rw_scan_kernel, TS=TS, C=C)
    return pl.pallas_call(
        kern,
        grid=(B, W // LANES, S // TS),
        in_specs=[blk] * 7 + [vblk] * 3,
        out_specs=blk,
        out_shape=jax.ShapeDtypeStruct((B, S, W), BF16),
        scratch_shapes=[pltpu.VMEM((LANES, LANES), F32)],
        compiler_params=_cparams(("parallel", "parallel", "arbitrary")),
        name="rw_scan",
    )(r, lw, kx, km, v, a, g, r_k.reshape(1, W), lnx_w.reshape(1, W), lnx_b.reshape(1, W))


def kernel(x, mem, norm1, norm_mem, w_mem_kv, w_o, norm2, w_ffn_in, w_ffn_out, nsa_w_in, nsa_gate_b,
           nsa_cmp_pos, nsa_cmp_w1, nsa_cmp_w2, rw_w_in, rw_mu, rw_w0, rw_w2, rw_a0, rw_a2, rw_g2,
           rw_k_k, rw_k_a, rw_r_k, rw_lnx_w, rw_lnx_b, final_norm):
    depth = norm1.shape[0]
    B, S, _ = x.shape
    ktm, vm = _mem_kv(mem, norm_mem, w_mem_kv)
    for i in range(depth):
        j = i // 2
        if i % 2 == 0:
            q, kc, vc, vs, vw, kst, kwt, qm, gates = _nsa_proj(x, norm1[i], nsa_w_in[j], nsa_gate_b[j])
            kcb, vcb = _compress(kc, vc, nsa_cmp_pos[j], nsa_cmp_w1[j], nsa_cmp_w2[j])
            mix = _nsa_attn(q, kcb, vcb, kst, vs, kwt, vw, gates)
        else:
            r, lw, kx, km, v, a, g, qm = _rw_prep(x, norm1[i], rw_w_in[j], rw_mu[j], rw_w0[j], rw_w2[j],
                                                  rw_a0[j], rw_a2[j], rw_g2[j], rw_k_k[j], rw_k_a[j])
            mix = _rw_scan(r, lw, kx, km, v, a, g, rw_r_k[j], rw_lnx_w[j], rw_lnx_b[j])
            mix = mix.reshape(B, 1, S, MIX_WIDTH)
        cross = _mem_attn(qm, ktm[i], vm[i])
        x = _post(x, mix, cross, w_o[i], norm2[i], w_ffn_in[i], w_ffn_out[i], final_norm,
                  final=(i == depth - 1))
    return x
```

```python
import functools
import math

import numpy as np
import jax
import jax.numpy as jnp
from jax import lax
from jax.experimental import pallas as pl
from jax.experimental.pallas import tpu as pltpu

F32 = jnp.float32
BF16 = jnp.bfloat16
HI = lax.Precision.HIGHEST

D_MODEL = 1024
HEAD_DIM = 64
MIX_WIDTH = 768
MIX_HEADS = 12
MEM_HEADS = 4
MEM_WIDTH = 256
KV_GROUPS = 4
GROUP_SIZE = 3
KV_WIDTH = 256
CMP_BLOCK = 32
CMP_STRIDE = 16
CMP_HIDDEN = 128
SEL_BLOCK = 64
SEL_TOPN = 16
WINDOW = 512
FORCE_SCORE = 1.0e4
DECAY_LORA = 64
ICLR_LORA = 64
GATE_LORA = 160
GN_EPS = 64e-5
FFN_HIDDEN = 2816
RMS_EPS = 1e-6
NEG_INF = -1e30

LANES = 128
VMEM_LIMIT = 56 * 1024 * 1024

_NT = (((1,), (1,)), ((), ()))
_TN = (((0,), (0,)), ((), ()))


def _mm(a, b, dims=None, hi=False):
    if hi:
        a, b, prec = a.astype(F32), b.astype(F32), HI
    else:
        a, b, prec = a.astype(BF16), b.astype(BF16), None
    if dims is None:
        return jnp.dot(a, b, preferred_element_type=F32, precision=prec)
    return lax.dot_general(a, b, dims, preferred_element_type=F32, precision=prec)


def _iota(shape, dim):
    return lax.broadcasted_iota(jnp.int32, shape, dim)


def _rms(x, g):
    ms = jnp.mean(x * x, axis=-1, keepdims=True)
    return x * lax.rsqrt(ms + RMS_EPS) * g


def _cparams(sem):
    return pltpu.CompilerParams(dimension_semantics=sem, vmem_limit_bytes=VMEM_LIMIT)


def _mem_kv_kernel(mem_ref, g_ref, wkt_ref, wv_ref, kt_ref, v_ref):
    mn = _rms(mem_ref[0], g_ref[0]).astype(BF16)
    kt = _mm(wkt_ref[0], mn, _NT)
    v = _mm(mn, wv_ref[0])
    rowh = _iota(kt.shape, 0) // HEAD_DIM
    colh = _iota(v.shape, 1) // HEAD_DIM
    for h in range(MEM_HEADS):
        kt_ref[0, 0, h] = jnp.where(rowh == h, kt, 0.0).astype(BF16)
        v_ref[0, 0, h] = jnp.where(colh == h, v, 0.0).astype(BF16)


def _mem_kv(mem, norm_mem, w_mem_kv):
    B, M, D = mem.shape
    L = norm_mem.shape[0]
    wkt = jnp.swapaxes(w_mem_kv[:, :, :MEM_WIDTH], 1, 2).astype(BF16)
    wv = w_mem_kv[:, :, MEM_WIDTH:].astype(BF16)
    return pl.pallas_call(
        _mem_kv_kernel,
        grid=(L, B),
        in_specs=[
            pl.BlockSpec((1, M, D), lambda l, b: (b, 0, 0)),
            pl.BlockSpec((1, 1, D), lambda l, b: (l, 0, 0)),
            pl.BlockSpec((1, MEM_WIDTH, D), lambda l, b: (l, 0, 0)),
            pl.BlockSpec((1, D, MEM_WIDTH), lambda l, b: (l, 0, 0)),
        ],
        out_specs=[
            pl.BlockSpec((1, 1, MEM_HEADS, MEM_WIDTH, M), lambda l, b: (l, b, 0, 0, 0)),
            pl.BlockSpec((1, 1, MEM_HEADS, M, MEM_WIDTH), lambda l, b: (l, b, 0, 0, 0)),
        ],
        out_shape=[
            jax.ShapeDtypeStruct((L, B, MEM_HEADS, MEM_WIDTH, M), BF16),
            jax.ShapeDtypeStruct((L, B, MEM_HEADS, M, MEM_WIDTH), BF16),
        ],
        compiler_params=_cparams(("parallel", "parallel")),
        name="mem_kv",
    )(mem, norm_mem.reshape(L, 1, D), wkt, wv)


def _mem_attn_kernel(q_ref, kt_ref, v_ref, o_ref):
    q = q_ref[0]
    acc = jnp.zeros(q.shape, F32)
    for h in range(MEM_HEADS):
        s = _mm(q, kt_ref[0, h])
        m = jnp.max(s, axis=-1, keepdims=True)
        p = jnp.exp(s - m)
        l = jnp.sum(p, axis=-1, keepdims=True)
        acc = acc + _mm(p * (1.0 / l), v_ref[0, h])
    o_ref[0] = acc.astype(BF16)


def _mem_attn(qm, ktm, vm):
    B, S, _ = qm.shape
    M = ktm.shape[-1]
    TM = min(512, S)
    return pl.pallas_call(
        _mem_attn_kernel,
        grid=(B, S // TM),
        in_specs=[
            pl.BlockSpec((1, TM, MEM_WIDTH), lambda b, s: (b, s, 0)),
            pl.BlockSpec((1, MEM_HEADS, MEM_WIDTH, M), lambda b, s: (b, 0, 0, 0)),
            pl.BlockSpec((1, MEM_HEADS, M, MEM_WIDTH), lambda b, s: (b, 0, 0, 0)),
        ],
        out_specs=pl.BlockSpec((1, TM, MEM_WIDTH), lambda b, s: (b, s, 0)),
        out_shape=jax.ShapeDtypeStruct((B, S, MEM_WIDTH), BF16),
        compiler_params=_cparams(("parallel", "parallel")),
        name="mem_attn",
    )(qm, ktm, vm)


def _post_kernel(x_ref, mix_ref, cross_ref, wom_ref, woc_ref, g2_ref, wg_ref, wu_ref, wout_ref,
                 gf_ref, o_ref, x1_scr, hn_scr, acc_scr, *, n_parts, n_h, final):
    h = pl.program_id(2)

    @pl.when(h == 0)
    def _():
        x1 = x_ref[0] + _mm(cross_ref[0], woc_ref[...])
        for p in range(n_parts):
            x1 = x1 + _mm(mix_ref[0, p], wom_ref[p])
        x1_scr[...] = x1
        hn_scr[...] = _rms(x1, g2_ref[...]).astype(BF16)
        acc_scr[...] = jnp.zeros(acc_scr.shape, F32)

    hn = hn_scr[...]
    gate = _mm(hn, wg_ref[...])
    up = _mm(hn, wu_ref[...])
    hid = gate * jax.nn.sigmoid(gate) * up
    acc_scr[...] += _mm(hid, wout_ref[...])

    @pl.when(h == n_h - 1)
    def _():
        y = x1_scr[...] + acc_scr[...]
        if final:
            y = _rms(y, gf_ref[...])
        o_ref[0] = y


def _post(x, mix, cross, w_o, norm2, w_ffn_in, w_ffn_out, final_norm, final):
    B, S, D = x.shape
    P, W = mix.shape[1], mix.shape[3]
    TM = min(1024, S)
    TH = 256
    NH = FFN_HIDDEN // TH
    wom = w_o[:MIX_WIDTH].reshape(P, W, D).astype(BF16)
    woc = w_o[MIX_WIDTH:].astype(BF16)
    wi = w_ffn_in.astype(BF16)
    wo = w_ffn_out.astype(BF16)
    kern = functools.partial(_post_kernel, n_parts=P, n_h=NH, final=final)
    return pl.pallas_call(
        kern,
        grid=(B, S // TM, NH),
        in_specs=[
            pl.BlockSpec((1, TM, D), lambda b, s, h: (b, s, 0)),
            pl.BlockSpec((1, P, TM, W), lambda b, s, h: (b, 0, s, 0)),
            pl.BlockSpec((1, TM, MEM_WIDTH), lambda b, s, h: (b, s, 0)),
            pl.BlockSpec((P, W, D), lambda b, s, h: (0, 0, 0)),
            pl.BlockSpec((MEM_WIDTH, D), lambda b, s, h: (0, 0)),
            pl.BlockSpec((1, D), lambda b, s, h: (0, 0)),
            pl.BlockSpec((D, TH), lambda b, s, h: (0, h)),
            pl.BlockSpec((D, TH), lambda b, s, h: (0, NH + h)),
            pl.BlockSpec((TH, D), lambda b, s, h: (h, 0)),
            pl.BlockSpec((1, D), lambda b, s, h: (0, 0)),
        ],
        out_specs=pl.BlockSpec((1, TM, D), lambda b, s, h: (b, s, 0)),
        out_shape=jax.ShapeDtypeStruct((B, S, D), F32),
        scratch_shapes=[
            pltpu.VMEM((TM, D), F32),
            pltpu.VMEM((TM, D), BF16),
            pltpu.VMEM((TM, D), F32),
        ],
        compiler_params=_cparams(("parallel", "parallel", "arbitrary")),
        name="post_ffn",
    )(x, mix, cross, wom, woc, norm2.reshape(1, D), wi, wi, wo, final_norm.reshape(1, D))


NSA_ROW_W = MIX_WIDTH + 4 * KV_WIDTH + MEM_WIDTH + LANES


def _nsa_proj_kernel(x_ref, g_ref, w_ref, wt_ref, gb_ref, q_ref, kc_ref, vc_ref, vs_ref, vw_ref,
                     kst_ref, kwt_ref, qm_ref, gate_ref):
    hn = _rms(x_ref[0], g_ref[...]).astype(BF16)
    res = _mm(hn, w_ref[...])
    for h in range(MIX_HEADS):
        q_ref[0, h] = res[:, h * HEAD_DIM:(h + 1) * HEAD_DIM].astype(BF16)
    off = MIX_WIDTH
    for ref in (kc_ref, vc_ref, vs_ref, vw_ref):
        for g in range(KV_GROUPS):
            ref[0, g] = res[:, off + g * HEAD_DIM: off + (g + 1) * HEAD_DIM].astype(BF16)
        off += KV_WIDTH
    qm_ref[0] = res[:, off:off + MEM_WIDTH].astype(BF16)
    off += MEM_WIDTH
    gates = jax.nn.sigmoid(res[:, off:off + 64] + gb_ref[...])
    for g in range(KV_GROUPS):
        gate_ref[0, g] = gates[:, g * 16:(g + 1) * 16]
    rt = _mm(wt_ref[...], hn, _NT)
    kst_ref[0] = rt[:KV_WIDTH].astype(BF16)
    kwt_ref[0] = rt[KV_WIDTH:].astype(BF16)


def _nsa_proj(x, norm1, w_in, gate_b):
    B, S, D = x.shape
    TM = min(512, S)
    G, H, Dh = KV_GROUPS, MIX_HEADS, HEAD_DIM
    scale = HEAD_DIM ** -0.5
    o = np.cumsum([0, MIX_WIDTH] + [KV_WIDTH] * 6 + [3 * MIX_HEADS, MEM_WIDTH])
    wq, wkc, wvc, wks, wvs, wkw, wvw, wgl, wqm = (w_in[:, o[i]:o[i + 1]] for i in range(9))
    wgl = wgl.reshape(D, 3, G, GROUP_SIZE).transpose(0, 2, 1, 3).reshape(D, G, 9)
    wgl = jnp.pad(wgl, ((0, 0), (0, 0), (0, 7))).reshape(D, 64)
    gb = gate_b.reshape(3, G, GROUP_SIZE).transpose(1, 0, 2).reshape(G, 9)
    gb = jnp.pad(gb, ((0, 0), (0, 7))).reshape(1, 64)
    w_row = jnp.concatenate(
        [wq * scale, wkc, wvc, wvs, wvw, wqm * scale, wgl, jnp.zeros((D, LANES - 64), F32)],
        axis=1).astype(BF16)
    w_t = jnp.concatenate([wks, wkw], axis=1).T.astype(BF16)
    head = lambda n: pl.BlockSpec((1, n, TM, Dh), lambda b, s: (b, 0, s, 0))
    return pl.pallas_call(
        _nsa_proj_kernel,
        grid=(B, S // TM),
        in_specs=[
            pl.BlockSpec((1, TM, D), lambda b, s: (b, s, 0)),
            pl.BlockSpec((1, D), lambda b, s: (0, 0)),
            pl.BlockSpec((D, NSA_ROW_W), lambda b, s: (0, 0)),
            pl.BlockSpec((2 * KV_WIDTH, D), lambda b, s: (0, 0)),
            pl.BlockSpec((1, 64), lambda b, s: (0, 0)),
        ],
        out_specs=[
            head(H), head(G), head(G), head(G), head(G),
            pl.BlockSpec((1, KV_WIDTH, TM), lambda b, s: (b, 0, s)),
            pl.BlockSpec((1, KV_WIDTH, TM), lambda b, s: (b, 0, s)),
            pl.BlockSpec((1, TM, MEM_WIDTH), lambda b, s: (b, s, 0)),
            pl.BlockSpec((1, G, TM, 16), lambda b, s: (b, 0, s, 0)),
        ],
        out_shape=[
            jax.ShapeDtypeStruct((B, H, S, Dh), BF16),
            jax.ShapeDtypeStruct((B, G, S, Dh), BF16),
            jax.ShapeDtypeStruct((B, G, S, Dh), BF16),
            jax.ShapeDtypeStruct((B, G, S, Dh), BF16),
            jax.ShapeDtypeStruct((B, G, S, Dh), BF16),
            jax.ShapeDtypeStruct((B, KV_WIDTH, S), BF16),
            jax.ShapeDtypeStruct((B, KV_WIDTH, S), BF16),
            jax.ShapeDtypeStruct((B, S, MEM_WIDTH), BF16),
            jax.ShapeDtypeStruct((B, G, S, 16), F32),
        ],
        compiler_params=_cparams(("parallel", "parallel")),
        name="nsa_proj",
    )(x, norm1.reshape(1, D), w_row, w_t, gb)


def _compress_kernel(kc_ref, vc_ref, w1_ref, w2_ref, pos_ref, kcb_ref, vcb_ref):
    half = CMP_STRIDE * HEAD_DIM
    for idx, (src, dst) in enumerate(((kc_ref, kcb_ref), (vc_ref, vcb_ref))):
        c = src[0, 0]
        ncp = c.shape[0]
        a = _mm(c, w1_ref[idx, :half])
        bm = _mm(c, w1_ref[idx, half:])
        bias = _mm(pos_ref[idx], w1_ref[idx])[0:1]
        hid = jax.nn.gelu(a + pltpu.roll(bm, ncp - 1, 0) + bias)
        out = _mm(hid, w2_ref[idx])
        row = _iota(out.shape, 0)
        dst[0, 0] = jnp.where(row < ncp - 1, out, 0.0).astype(BF16)


def _compress(kc, vc, cmp_pos, cmp_w1, cmp_w2):
    B, G, S, Dh = kc.shape
    NCP = S // CMP_STRIDE
    kcr = kc.reshape(B, G, NCP, CMP_STRIDE * Dh)
    vcr = vc.reshape(B, G, NCP, CMP_STRIDE * Dh)
    w1 = cmp_w1.reshape(2, CMP_BLOCK * Dh, CMP_HIDDEN).astype(BF16)
    w2 = cmp_w2.astype(BF16)
    pos = jnp.broadcast_to(cmp_pos.reshape(2, 1, CMP_BLOCK * Dh), (2, 8, CMP_BLOCK * Dh)).astype(BF16)
    blk = pl.BlockSpec((1, 1, NCP, CMP_STRIDE * Dh), lambda b, g: (b, g, 0, 0))
    oblk = pl.BlockSpec((1, 1, NCP, Dh), lambda b, g: (b, g, 0, 0))
    return pl.pallas_call(
        _compress_kernel,
        grid=(B, G),
        in_specs=[
            blk, blk,
            pl.BlockSpec((2, CMP_BLOCK * Dh, CMP_HIDDEN), lambda b, g: (0, 0, 0)),
            pl.BlockSpec((2, CMP_HIDDEN, Dh), lambda b, g: (0, 0, 0)),
            pl.BlockSpec((2, 8, CMP_BLOCK * Dh), lambda b, g: (0, 0, 0)),
        ],
        out_specs=[oblk, oblk],
        out_shape=[jax.ShapeDtypeStruct((B, G, NCP, Dh), BF16)] * 2,
        compiler_params=_cparams(("parallel", "parallel")),
        name="nsa_compress",
    )(kcr, vcr, w1, w2, pos)


def _alibi_slopes(n):
    def pow2(m):
        start = 2.0 ** (-8.0 / m)
        return [start ** (i + 1) for i in range(m)]
    c = 2 ** int(math.floor(math.log2(n)))
    s = pow2(c)
    if c < n:
        s = s + pow2(2 * c)[0::2][: n - c]
    return np.asarray(s, dtype=np.float32)


def _nsa_attn_kernel(slope_ref, q_ref, kcb_ref, vcb_ref, ovt_ref, kst_ref, vs_ref, kwt_ref, vw_ref,
                     e_ref, gate_ref, o_ref, score_scr, m_scr, l_scr, acc_scr, *, TQ, TK, S, n_sel):
    R, Dh = GROUP_SIZE, HEAD_DIM
    g = pl.program_id(1)
    t0 = pl.program_id(2) * TQ
    NCP, NB = S // CMP_STRIDE, S // SEL_BLOCK
    qs = q_ref[0].reshape(R * TQ, Dh)
    slopes = [slope_ref[R * g + r] for r in range(R)]
    tq_col = t0 + _iota((TQ, 1), 0)

    sc = _mm(qs, kcb_ref[0, 0], _NT)
    n_row = _iota((1, NCP), 1)
    cend = n_row * CMP_STRIDE + (CMP_BLOCK - 1)
    mask_c = (cend <= tq_col) & (n_row < NCP - 1)
    dpos_c = (cend - t0).astype(F32)
    psum = jnp.zeros((TQ, NCP), F32)
    pcs = []
    for r in range(R):
        s = jnp.where(mask_c, sc[r * TQ:(r + 1) * TQ] + slopes[r] * dpos_c, NEG_INF)
        m = jnp.max(s, axis=-1, keepdims=True)
        p = jnp.where(mask_c, jnp.exp(s - m), 0.0)
        l = jnp.sum(p, axis=-1, keepdims=True)
        p = p * jnp.where(l > 0.0, 1.0 / l, 0.0)
        psum = psum + p
        pcs.append(p)
    o_c = _mm(jnp.concatenate(pcs, axis=0), vcb_ref[0, 0])

    imp_t = _mm(ovt_ref[...], psum, _NT, hi=True)
    tb = lax.shift_right_logical(t0 + _iota((1, TQ), 1), 6)
    j_col = _iota((NB, 1), 0)
    valid = j_col <= tb
    forced = (j_col == 0) | (j_col == tb) | (j_col == tb - 1)
    score = jnp.where(valid, jnp.where(forced, FORCE_SCORE, imp_t), -jnp.inf)
    score_scr[...] = score
    n_valid = (t0 + TQ - 1) // SEL_BLOCK + 1

    def rank_body(i, cnt):
        row = score_scr[pl.ds(i, 1), :]
        tie = jnp.where(i < j_col, 1.0, 0.0)
        return cnt + jnp.where(row > score, 1.0, jnp.where(row == score, tie, 0.0))

    rank = lax.fori_loop(0, n_valid, rank_body, jnp.zeros((NB, TQ), F32))
    sel_t = jnp.where(valid, jnp.where(rank < n_sel, 1.0, 0.0), 0.0)
    sel = sel_t.T.astype(BF16)

    m_scr[...] = jnp.full(m_scr.shape, NEG_INF, F32)
    l_scr[...] = jnp.zeros(l_scr.shape, F32)
    acc_scr[...] = jnp.zeros(acc_scr.shape, F32)
    n_kt = (t0 + TQ - 1) // TK + 1

    def sel_body(kt, carry):
        k0 = pl.multiple_of(kt * TK, TK)
        s_all = _mm(qs, kst_ref[0, :, pl.ds(k0, TK)])
        mf = _mm(sel, e_ref[:, pl.ds(k0, TK)])
        kpos = k0 + _iota((1, TK), 1)
        mask = (mf > 0.5) & (kpos <= tq_col)
        dpos = (kpos - t0).astype(F32)
        v = vs_ref[0, 0, pl.ds(k0, TK), :]
        for r in range(R):
            rows = slice(r * TQ, (r + 1) * TQ)
            s = jnp.where(mask, s_all[rows] + slopes[r] * dpos, NEG_INF)
            m_old = m_scr[rows]
            m_new = jnp.maximum(m_old, jnp.max(s, axis=-1, keepdims=True))
            alpha = jnp.exp(m_old - m_new)
            p = jnp.where(mask, jnp.exp(s - m_new), 0.0)
            l_scr[rows] = alpha * l_scr[rows] + jnp.sum(p, axis=-1, keepdims=True)
            acc_scr[rows] = alpha * acc_scr[rows] + _mm(p, v)
            m_scr[rows] = m_new
        return carry

    lax.fori_loop(0, n_kt, sel_body, 0)

    WK = WINDOW + TQ
    w0 = pl.multiple_of(jnp.maximum(t0 - WINDOW, 0), LANES)
    sw = _mm(qs, kwt_ref[0, :, pl.ds(w0, WK)])
    kpos_w = w0 + _iota((1, WK), 1)
    dist_w = tq_col - kpos_w
    mask_w = (dist_w >= 0) & (dist_w < WINDOW)
    dpos_w = (kpos_w - t0).astype(F32)
    pws = []
    for r in range(R):
        s = jnp.where(mask_w, sw[r * TQ:(r + 1) * TQ] + slopes[r] * dpos_w, NEG_INF)
        m = jnp.max(s, axis=-1, keepdims=True)
        p = jnp.where(mask_w, jnp.exp(s - m), 0.0)
        pws.append(p * (1.0 / jnp.sum(p, axis=-1, keepdims=True)))
    o_w = _mm(jnp.concatenate(pws, axis=0), vw_ref[0, 0, pl.ds(w0, WK), :])

    o_s = acc_scr[...] * (1.0 / l_scr[...])
    gv = gate_ref[0, 0]
    for r in range(R):
        rows = slice(r * TQ, (r + 1) * TQ)
        out = (gv[:, r:r + 1] * o_c[rows] + gv[:, R + r:R + r + 1] * o_s[rows]
               + gv[:, 2 * R + r:2 * R + r + 1] * o_w[rows])
        o_ref[0, 0, :, r * Dh:(r + 1) * Dh] = out.astype(BF16)


def _nsa_attn(q, kcb, vcb, kst, vs, kwt, vw, gates):
    B, H, S, Dh = q.shape
    G, R = KV_GROUPS, GROUP_SIZE
    TQ = 128
    TK = min(512, S)
    NCP, NB = S // CMP_STRIDE, S // SEL_BLOCK
    n_sel = min(SEL_TOPN, NB)
    assert S % TK == 0 and S >= WINDOW + TQ
    slopes = jnp.asarray(_alibi_slopes(MIX_HEADS))
    cs = np.arange(NCP) * CMP_STRIDE
    ss = np.arange(NB) * SEL_BLOCK
    ov = (cs[:, None] <= ss[None, :] + SEL_BLOCK - 1) & (cs[:, None] + CMP_BLOCK - 1 >= ss[None, :])
    ov[NCP - 1] = False
    ovt = jnp.asarray(ov.T.astype(np.float32))
    e = jnp.asarray((np.arange(S)[None, :] // SEL_BLOCK == np.arange(NB)[:, None]).astype(np.float32), BF16)
    kern = functools.partial(_nsa_attn_kernel, TQ=TQ, TK=TK, S=S, n_sel=n_sel)
    return pl.pallas_call(
        kern,
        grid=(B, G, S // TQ),
        in_specs=[
            pl.BlockSpec(memory_space=pltpu.SMEM),
            pl.BlockSpec((1, R, TQ, Dh), lambda b, g, i: (b, g, i, 0)),
            pl.BlockSpec((1, 1, NCP, Dh), lambda b, g, i: (b, g, 0, 0)),
            pl.BlockSpec((1, 1, NCP, Dh), lambda b, g, i: (b, g, 0, 0)),
            pl.BlockSpec((NB, NCP), lambda b, g, i: (0, 0)),
            pl.BlockSpec((1, Dh, S), lambda b, g, i: (b, g, 0)),
            pl.BlockSpec((1, 1, S, Dh), lambda b, g, i: (b, g, 0, 0)),
            pl.BlockSpec((1, Dh, S), lambda b, g, i: (b, g, 0)),
            pl.BlockSpec((1, 1, S, Dh), lambda b, g, i: (b, g, 0, 0)),
            pl.BlockSpec((NB, S), lambda b, g, i: (0, 0)),
            pl.BlockSpec((1, 1, TQ, 16), lambda b, g, i: (b, g, i, 0)),
        ],
        out_specs=pl.BlockSpec((1, 1, TQ, R * Dh), lambda b, g, i: (b, g, i, 0)),
        out_shape=jax.ShapeDtypeStruct((B, G, S, R * Dh), BF16),
        scratch_shapes=[
            pltpu.VMEM((NB, TQ), F32),
            pltpu.VMEM((R * TQ, 1), F32),
            pltpu.VMEM((R * TQ, 1), F32),
            pltpu.VMEM((R * TQ, Dh), F32),
        ],
        compiler_params=_cparams(("parallel", "parallel", "arbitrary")),
        name="nsa_attn",
    )(slopes, q, kcb, vcb, ovt, kst, vs, kwt, vw, e, gates)


RW_Z_W = 3 * MIX_WIDTH + 2 * LANES + 2 * LANES
RW_ROW_W = RW_Z_W + MEM_WIDTH


def _rw_prep_kernel(x_ref, g_ref, w_ref, mu_ref, w0_ref, w2_ref, a0_ref, a2_ref, g2_ref, kk_ref,
                    ka_ref, r_ref, lw_ref, kx_ref, km_ref, v_ref, a_ref, go_ref, qm_ref, carry_scr):
    W = MIX_WIDTH

    @pl.when(pl.program_id(1) == 0)
    def _():
        carry_scr[...] = jnp.zeros(carry_scr.shape, F32)

    hn = _rms(x_ref[0], g_ref[...]).astype(BF16)
    res = _mm(hn, w_ref[...])
    qm_ref[0] = res[:, RW_Z_W:].astype(BF16)
    z = res[:, :RW_Z_W]
    tm = z.shape[0]
    zprev = jnp.where(_iota((tm, 1), 0) == 0, carry_scr[0:1, :], pltpu.roll(z, 1, 0))
    carry_scr[0:1, :] = z[tm - 1:tm, :]
    z = z + (zprev - z) * mu_ref[...]
    r, k, v = z[:, :W], z[:, W:2 * W], z[:, 2 * W:3 * W]
    zw = z[:, 3 * W:3 * W + LANES]
    za = z[:, 3 * W + LANES:3 * W + 2 * LANES]
    zg = z[:, 3 * W + 2 * LANES:]
    w_log = -jax.nn.softplus(-(w0_ref[...] + _mm(jnp.tanh(zw), w2_ref[...]))) - 0.5
    a = jax.nn.sigmoid(a0_ref[...] + _mm(za, a2_ref[...]))
    r_ref[0] = r
    lw_ref[0] = -jnp.exp(w_log)
    kx_ref[0] = k * kk_ref[...]
    km_ref[0] = k * (1.0 + (a - 1.0) * ka_ref[...])
    v_ref[0] = v
    a_ref[0] = a
    go_ref[0] = _mm(jax.nn.sigmoid(zg), g2_ref[...])


def _pad_rows(w, n):
    return jnp.pad(w, ((0, n - w.shape[0]), (0, 0)))


def _rw_prep(x, norm1, w_in, mu, w0, w2, a0, a2, g2, k_k, k_a):
    B, S, D = x.shape
    W = MIX_WIDTH
    TM = min(256, S)
    o = np.cumsum([0, 3 * W, DECAY_LORA, ICLR_LORA, GATE_LORA, MEM_WIDTH])
    seg = [w_in[:, o[i]:o[i + 1]] for i in range(5)]
    padc = lambda w, n: jnp.pad(w, ((0, 0), (0, n - w.shape[1])))
    w_row = jnp.concatenate(
        [seg[0], padc(seg[1], LANES), padc(seg[2], LANES), padc(seg[3], 2 * LANES),
         seg[4] * HEAD_DIM ** -0.5], axis=1).astype(BF16)
    mus = [mu[o[i]:o[i + 1]] for i in range(4)]
    padv = lambda v, n: jnp.pad(v, (0, n - v.shape[0]))
    mu_p = jnp.concatenate([mus[0], padv(mus[1], LANES), padv(mus[2], LANES),
                            padv(mus[3], 2 * LANES)]).reshape(1, RW_Z_W)
    vec = lambda v: v.reshape(1, W)
    full = lambda a: pl.BlockSpec(a.shape, lambda b, s: (0,) * a.ndim)
    args = [norm1.reshape(1, D), w_row, mu_p, vec(w0), _pad_rows(w2, LANES).astype(BF16), vec(a0),
            _pad_rows(a2, LANES).astype(BF16), _pad_rows(g2, 2 * LANES).astype(BF16), vec(k_k), vec(k_a)]
    oblk = pl.BlockSpec((1, TM, W), lambda b, s: (b, s, 0))
    return pl.pallas_call(
        _rw_prep_kernel,
        grid=(B, S // TM),
        in_specs=[pl.BlockSpec((1, TM, D), lambda b, s: (b, s, 0))] + [full(a) for a in args],
        out_specs=[oblk] * 7 + [pl.BlockSpec((1, TM, MEM_WIDTH), lambda b, s: (b, s, 0))],
        out_shape=[jax.ShapeDtypeStruct((B, S, W), F32)] * 7
        + [jax.ShapeDtypeStruct((B, S, MEM_WIDTH), BF16)],
        scratch_shapes=[pltpu.VMEM((8, RW_Z_W), F32)],
        compiler_params=_cparams(("parallel", "arbitrary")),
        name="rw_prep",
    )(x, *args)


def _split2(x):
    hi = x.astype(BF16)
    return hi, (x - hi.astype(F32)).astype(BF16)


def _mm3(a, b):
    dot = functools.partial(jnp.dot, preferred_element_type=F32)
    return dot(a[0], b[0]) + dot(a[0], b[1]) + dot(a[1], b[0])


def _each(f, *lists):
    return [f(*args) for args in zip(*lists)]


def _tri_inverse(a2s, eye, same16, same32, same64):
    ds = _each(lambda a2: jnp.where(same16, a2, 0.0), a2s)
    ps = _each(lambda d: eye + d, ds)
    dss = _each(_split2, ds)
    for _ in range(3):
        dss = _each(lambda d: _split2(_mm3(d, d)), dss)
        ps = _each(lambda p, d: p + _mm3(_split2(p), d), ps, dss)
    for lo, hi_ in ((same16, same32), (same32, same64)):
        sel = hi_ & jnp.logical_not(lo)
        offs = _each(lambda a2: _split2(jnp.where(sel, a2, 0.0)), a2s)
        pss = _each(_split2, ps)
        mids = _each(lambda p, o: _split2(_mm3(p, o)), pss, offs)
        ps = _each(lambda p, m, q: p + _mm3(m, q), ps, mids, pss)
    return ps


def _rw_scan_kernel(r_ref, lw_ref, kx_ref, km_ref, v_ref, a_ref, g_ref, rk_ref, lnw_ref, lnb_ref,
                    o_ref, st_scr, *, TS, C):
    Dh = HEAD_DIM

    @pl.when(pl.program_id(1) == 0)
    def _():
        st_scr[...] = jnp.zeros(st_scr.shape, F32)

    head0 = _iota((1, LANES), 1) < Dh
    ltri = jnp.where(_iota((C, C), 1) <= _iota((C, C), 0), 1.0, 0.0)
    col2 = _iota((C, LANES), 1) & (Dh - 1)
    row2 = _iota((C, LANES), 0)
    m_incl = col2 <= row2
    m_strict = col2 < row2
    r128 = _iota((LANES, LANES), 0)
    c128 = _iota((LANES, LANES), 1)
    eye = jnp.where(r128 == c128, 1.0, 0.0)
    same16 = (r128 >> 4) == (c128 >> 4)
    same32 = (r128 >> 5) == (c128 >> 5)
    same64 = (r128 >> 6) == (c128 >> 6)
    zeros = jnp.zeros((C, LANES), F32)
    ltri = ltri.astype(BF16)

    def hsum(x):
        s0 = jnp.sum(jnp.where(head0, x, 0.0), axis=-1, keepdims=True)
        s1 = jnp.sum(jnp.where(head0, 0.0, x), axis=-1, keepdims=True)
        return jnp.where(head0, s0, s1)

    n_pairs = MIX_WIDTH // LANES
    lns = [slice(pi * LANES, (pi + 1) * LANES) for pi in range(n_pairs)]
    pis = list(range(n_pairs))

    def cumsum_decay(lw):
        l1 = lw.astype(BF16)
        l2 = (lw - l1.astype(F32)).astype(BF16)
        l3 = (lw - l1.astype(F32) - l2.astype(F32)).astype(BF16)
        cum3 = jnp.dot(ltri, jnp.concatenate([l1, l2, l3], axis=1), preferred_element_type=F32)
        return cum3[:, :LANES] + cum3[:, LANES:2 * LANES] + cum3[:, 2 * LANES:]

    def scaled(ln, cum, sl):
        r, lw, kx, km, a = r_ref[0, sl, ln], lw_ref[0, sl, ln], kx_ref[0, sl, ln], km_ref[0, sl, ln], a_ref[0, sl, ln]
        kk = kx / jnp.maximum(jnp.sqrt(hsum(kx * kx)), 1e-12)
        p_in = jnp.exp(cum)
        at = -kk * jnp.exp(cum - lw)
        rt = r * p_in
        p_inv = jnp.exp(-cum)
        bk = jnp.concatenate([kk * a * p_inv, km * p_inv], axis=0)
        at0, at1 = jnp.where(head0, at, 0.0), jnp.where(head0, 0.0, at)
        rt0, rt1 = jnp.where(head0, rt, 0.0), jnp.where(head0, 0.0, rt)
        lhs = jnp.concatenate([at0, at1, rt0, rt1], axis=0)
        return lhs, bk, rt, p_in[C - 1:C, :]

    def split_aa(aa):
        aa0 = jnp.where(m_strict, aa[0:C], 0.0)
        aa1 = pltpu.roll(jnp.where(m_strict, aa[C:2 * C], 0.0), Dh, 1)
        ar0 = jnp.where(m_incl, aa[2 * C:3 * C], 0.0)
        ar1 = jnp.where(m_incl, aa[3 * C:4 * C], 0.0)
        a2 = jnp.concatenate([jnp.where(head0, aa0, 0.0), jnp.where(head0, 0.0, aa1)], axis=0)
        return aa0, aa1, ar0, ar1, a2

    def epilogue(ln, y, sl):
        r, km, v = r_ref[0, sl, ln], km_ref[0, sl, ln], v_ref[0, sl, ln]
        mean = hsum(y) * (1.0 / Dh)
        d = y - mean
        var = hsum(d * d) * (1.0 / Dh)
        yn = d * lax.rsqrt(var + GN_EPS) * lnw_ref[:, ln] + lnb_ref[:, ln]
        bonus = hsum(r * km * rk_ref[:, ln]) * v
        o_ref[0, sl, ln] = ((yn + bonus) * g_ref[0, sl, ln]).astype(BF16)

    def chunk(c, carry):
        sl = pl.ds(pl.multiple_of(c * C, C), C)
        cums = _each(lambda ln: cumsum_decay(lw_ref[0, sl, ln]), lns)
        lhss, bks, rts, pcs = zip(*_each(lambda ln, cum: scaled(ln, cum, sl), lns, cums))
        aas = _each(lambda lhs, bk: _mm(lhs, bk, _NT), lhss, bks)
        aa0s, aa1s, ar0s, ar1s, a2s = zip(*_each(split_aa, aas))
        t2s = _tri_inverse(a2s, eye, same16, same32, same64)
        vs = _each(lambda ln: v_ref[0, sl, ln], lns)
        x0s = _each(lambda aa0, v: _mm(aa0, jnp.concatenate([zeros, v], axis=0)), aa0s, vs)
        x1s = _each(lambda aa1, v: _mm(aa1, jnp.concatenate([v, zeros], axis=0)), aa1s, vs)
        wus = _each(lambda t2, lhs, x0, x1: _mm(t2, jnp.concatenate(
            [lhs[:2 * C], jnp.concatenate([x0, x1], axis=0)], axis=1)), t2s, lhss, x0s, x1s)
        sts = _each(lambda pi: st_scr[pi], pis)
        us = _each(lambda wu, st: _mm(wu[:C, :LANES] + wu[C:, :LANES], st, _NT)
                   + jnp.where(head0, wu[:C, LANES:], wu[C:, LANES:]), wus, sts)
        uvs = _each(lambda u, v: jnp.concatenate([u, v], axis=0), us, vs)
        ys = _each(lambda rt, st, ar0, ar1, uv: _mm(rt, st, _NT)
                   + jnp.where(head0, _mm(ar0, uv), _mm(ar1, uv)), rts, sts, ar0s, ar1s, uvs)
        new = _each(lambda st, pc, uv, bk: st * pc + jnp.where(same64, _mm(uv, bk * pc, _TN), 0.0),
                    sts, pcs, uvs, bks)
        for pi in pis:
            st_scr[pi] = new[pi]
        _each(lambda ln, y: epilogue(ln, y, sl), lns, ys)
        return carry

    lax.fori_loop(0, TS // C, chunk, 0)


def _rw_scan(r, lw, kx, km, v, a, g, r_k, lnx_w, lnx_b):
    B, S, W = r.shape
    TS = min(256, S)
    C = 64
    blk = pl.BlockSpec((1, TS, W), lambda b, s: (b, s, 0))
    vblk = pl.BlockSpec((1, W), lambda b, s: (0, 0))
    kern = functools.partial(_rw_scan_kernel, TS=TS, C=C)
    return pl.pallas_call(
        kern,
        grid=(B, S // TS),
        in_specs=[blk] * 7 + [vblk] * 3,
        out_specs=blk,
        out_shape=jax.ShapeDtypeStruct((B, S, W), BF16),
        scratch_shapes=[pltpu.VMEM((W // LANES, LANES, LANES), F32)],
        compiler_params=_cparams(("parallel", "arbitrary")),
        name="rw_scan",
    )(r, lw, kx, km, v, a, g, r_k.reshape(1, W), lnx_w.reshape(1, W), lnx_b.reshape(1, W))


def kernel(x, mem, norm1, norm_mem, w_mem_kv, w_o, norm2, w_ffn_in, w_ffn_out, nsa_w_in, nsa_gate_b,
           nsa_cmp_pos, nsa_cmp_w1, nsa_cmp_w2, rw_w_in, rw_mu, rw_w0, rw_w2, rw_a0, rw_a2, rw_g2,
           rw_k_k, rw_k_a, rw_r_k, rw_lnx_w, rw_lnx_b, final_norm):
    depth = norm1.shape[0]
    B, S, _ = x.shape
    ktm, vm = _mem_kv(mem, norm_mem, w_mem_kv)
    for i in range(depth):
        j = i // 2
        if i % 2 == 0:
            q, kc, vc, vs, vw, kst, kwt, qm, gates = _nsa_proj(x, norm1[i], nsa_w_in[j], nsa_gate_b[j])
            kcb, vcb = _compress(kc, vc, nsa_cmp_pos[j], nsa_cmp_w1[j], nsa_cmp_w2[j])
            mix = _nsa_attn(q, kcb, vcb, kst, vs, kwt, vw, gates)
        else:
            r, lw, kx, km, v, a, g, qm = _rw_prep(x, norm1[i], rw_w_in[j], rw_mu[j], rw_w0[j], rw_w2[j],
                                                  rw_a0[j], rw_a2[j], rw_g2[j], rw_k_k[j], rw_k_a[j])
            mix = _rw_scan(r, lw, kx, km, v, a, g, rw_r_k[j], rw_lnx_w[j], rw_lnx_b[j])
            mix = mix.reshape(B, 1, S, MIX_WIDTH)
        cross = _mem_attn(qm, ktm[i], vm[i])
        x = _post(x, mix, cross, w_o[i], norm2[i], w_ffn_in[i], w_ffn_out[i], final_norm,
                  final=(i == depth - 1))
    return x
```

```python
import functools
import math

import numpy as np
import jax
import jax.numpy as jnp
from jax import lax
from jax.experimental import pallas as pl
from jax.experimental.pallas import tpu as pltpu

F32 = jnp.float32
BF16 = jnp.bfloat16
HI = lax.Precision.HIGHEST

D_MODEL = 1024
HEAD_DIM = 64
MIX_WIDTH = 768
MIX_HEADS = 12
MEM_HEADS = 4
MEM_WIDTH = 256
KV_GROUPS = 4
GROUP_SIZE = 3
KV_WIDTH = 256
CMP_BLOCK = 32
CMP_STRIDE = 16
CMP_HIDDEN = 128
SEL_BLOCK = 64
SEL_TOPN = 16
WINDOW = 512
FORCE_SCORE = 1.0e4
DECAY_LORA = 64
ICLR_LORA = 64
GATE_LORA = 160
GN_EPS = 64e-5
FFN_HIDDEN = 2816
RMS_EPS = 1e-6
NEG_INF = -1e30
LOG2E = 1.4426950408889634

LANES = 128
VMEM_LIMIT = 56 * 1024 * 1024

_NT = (((1,), (1,)), ((), ()))
_TN = (((0,), (0,)), ((), ()))


def _mm(a, b, dims=None, hi=False):
    if hi:
        a, b, prec = a.astype(F32), b.astype(F32), HI
    else:
        a, b, prec = a.astype(BF16), b.astype(BF16), None
    if dims is None:
        return jnp.dot(a, b, preferred_element_type=F32, precision=prec)
    return lax.dot_general(a, b, dims, preferred_element_type=F32, precision=prec)


def _iota(shape, dim):
    return lax.broadcasted_iota(jnp.int32, shape, dim)


def _rms(x, g):
    ms = jnp.mean(x * x, axis=-1, keepdims=True)
    return x * lax.rsqrt(ms + RMS_EPS) * g


def _cparams(sem):
    return pltpu.CompilerParams(dimension_semantics=sem, vmem_limit_bytes=VMEM_LIMIT)


def _mem_kv_kernel(mem_ref, g_ref, wkt_ref, wv_ref, kt_ref, v_ref):
    mn = _rms(mem_ref[0], g_ref[0]).astype(BF16)
    kt = _mm(wkt_ref[0], mn, _NT)
    v = _mm(mn, wv_ref[0])
    rowh = _iota(kt.shape, 0) // HEAD_DIM
    colh = _iota(v.shape, 1) // HEAD_DIM
    for h in range(MEM_HEADS):
        kt_ref[0, 0, h] = jnp.where(rowh == h, kt, 0.0).astype(BF16)
        v_ref[0, 0, h] = jnp.where(colh == h, v, 0.0).astype(BF16)


def _mem_kv(mem, norm_mem, w_mem_kv):
    B, M, D = mem.shape
    L = norm_mem.shape[0]
    wkt = jnp.swapaxes(w_mem_kv[:, :, :MEM_WIDTH], 1, 2).astype(BF16)
    wv = w_mem_kv[:, :, MEM_WIDTH:].astype(BF16)
    return pl.pallas_call(
        _mem_kv_kernel,
        grid=(L, B),
        in_specs=[
            pl.BlockSpec((1, M, D), lambda l, b: (b, 0, 0)),
            pl.BlockSpec((1, 1, D), lambda l, b: (l, 0, 0)),
            pl.BlockSpec((1, MEM_WIDTH, D), lambda l, b: (l, 0, 0)),
            pl.BlockSpec((1, D, MEM_WIDTH), lambda l, b: (l, 0, 0)),
        ],
        out_specs=[
            pl.BlockSpec((1, 1, MEM_HEADS, MEM_WIDTH, M), lambda l, b: (l, b, 0, 0, 0)),
            pl.BlockSpec((1, 1, MEM_HEADS, M, MEM_WIDTH), lambda l, b: (l, b, 0, 0, 0)),
        ],
        out_shape=[
            jax.ShapeDtypeStruct((L, B, MEM_HEADS, MEM_WIDTH, M), BF16),
            jax.ShapeDtypeStruct((L, B, MEM_HEADS, M, MEM_WIDTH), BF16),
        ],
        compiler_params=_cparams(("parallel", "parallel")),
        name="mem_kv",
    )(mem, norm_mem.reshape(L, 1, D), wkt, wv)


def _mem_attn_kernel(q_ref, kt_ref, v_ref, o_ref):
    q = q_ref[0]
    acc = jnp.zeros(q.shape, F32)
    for h in range(MEM_HEADS):
        s = _mm(q, kt_ref[0, h])
        m = jnp.max(s, axis=-1, keepdims=True)
        p = jnp.exp(s - m)
        l = jnp.sum(p, axis=-1, keepdims=True)
        acc = acc + _mm(p * (1.0 / l), v_ref[0, h])
    o_ref[0] = acc.astype(BF16)


def _mem_attn(qm, ktm, vm):
    B, S, _ = qm.shape
    M = ktm.shape[-1]
    TM = min(512, S)
    return pl.pallas_call(
        _mem_attn_kernel,
        grid=(B, S // TM),
        in_specs=[
            pl.BlockSpec((1, TM, MEM_WIDTH), lambda b, s: (b, s, 0)),
            pl.BlockSpec((1, MEM_HEADS, MEM_WIDTH, M), lambda b, s: (b, 0, 0, 0)),
            pl.BlockSpec((1, MEM_HEADS, M, MEM_WIDTH), lambda b, s: (b, 0, 0, 0)),
        ],
        out_specs=pl.BlockSpec((1, TM, MEM_WIDTH), lambda b, s: (b, s, 0)),
        out_shape=jax.ShapeDtypeStruct((B, S, MEM_WIDTH), BF16),
        compiler_params=_cparams(("parallel", "parallel")),
        name="mem_attn",
    )(qm, ktm, vm)


def _post_kernel(x_ref, mix_ref, cross_ref, wom_ref, woc_ref, g2_ref, wg_ref, wu_ref, wout_ref,
                 gf_ref, o_ref, x1_scr, hn_scr, acc_scr, *, n_parts, n_h, final):
    h = pl.program_id(2)

    @pl.when(h == 0)
    def _():
        x1 = x_ref[0] + _mm(cross_ref[0], woc_ref[...])
        for p in range(n_parts):
            x1 = x1 + _mm(mix_ref[0, p], wom_ref[p])
        x1_scr[...] = x1
        hn_scr[...] = _rms(x1, g2_ref[...]).astype(BF16)
        acc_scr[...] = jnp.zeros(acc_scr.shape, F32)

    hn = hn_scr[...]
    gate = _mm(hn, wg_ref[...])
    up = _mm(hn, wu_ref[...])
    hid = gate * jax.nn.sigmoid(gate) * up
    acc_scr[...] += _mm(hid, wout_ref[...])

    @pl.when(h == n_h - 1)
    def _():
        y = x1_scr[...] + acc_scr[...]
        if final:
            y = _rms(y, gf_ref[...])
        o_ref[0] = y


def _post(x, mix, cross, w_o, norm2, w_ffn_in, w_ffn_out, final_norm, final):
    B, S, D = x.shape
    P, W = mix.shape[1], mix.shape[3]
    TM = min(1024, S)
    TH = 256
    NH = FFN_HIDDEN // TH
    wom = w_o[:MIX_WIDTH].reshape(P, W, D).astype(BF16)
    woc = w_o[MIX_WIDTH:].astype(BF16)
    wi = w_ffn_in.astype(BF16)
    wo = w_ffn_out.astype(BF16)
    kern = functools.partial(_post_kernel, n_parts=P, n_h=NH, final=final)
    return pl.pallas_call(
        kern,
        grid=(B, S // TM, NH),
        in_specs=[
            pl.BlockSpec((1, TM, D), lambda b, s, h: (b, s, 0)),
            pl.BlockSpec((1, P, TM, W), lambda b, s, h: (b, 0, s, 0)),
            pl.BlockSpec((1, TM, MEM_WIDTH), lambda b, s, h: (b, s, 0)),
            pl.BlockSpec((P, W, D), lambda b, s, h: (0, 0, 0)),
            pl.BlockSpec((MEM_WIDTH, D), lambda b, s, h: (0, 0)),
            pl.BlockSpec((1, D), lambda b, s, h: (0, 0)),
            pl.BlockSpec((D, TH), lambda b, s, h: (0, h)),
            pl.BlockSpec((D, TH), lambda b, s, h: (0, NH + h)),
            pl.BlockSpec((TH, D), lambda b, s, h: (h, 0)),
            pl.BlockSpec((1, D), lambda b, s, h: (0, 0)),
        ],
        out_specs=pl.BlockSpec((1, TM, D), lambda b, s, h: (b, s, 0)),
        out_shape=jax.ShapeDtypeStruct((B, S, D), F32),
        scratch_shapes=[
            pltpu.VMEM((TM, D), F32),
            pltpu.VMEM((TM, D), BF16),
            pltpu.VMEM((TM, D), F32),
        ],
        compiler_params=_cparams(("parallel", "parallel", "arbitrary")),
        name="post_ffn",
    )(x, mix, cross, wom, woc, norm2.reshape(1, D), wi, wi, wo, final_norm.reshape(1, D))


NSA_ROW_W = MIX_WIDTH + 4 * KV_WIDTH + MEM_WIDTH + LANES


def _nsa_proj_kernel(x_ref, g_ref, w_ref, wt_ref, gb_ref, qc_ref, q_ref, kc_ref, vc_ref, vs_ref, vw_ref,
                     kst_ref, kwt_ref, qm_ref, gate_ref):
    hn = _rms(x_ref[0], g_ref[...]).astype(BF16)
    res = _mm(hn, w_ref[...])
    tm = res.shape[0]
    for h in range(MIX_HEADS):
        q_ref[0, h] = jnp.concatenate(
            [res[:, h * HEAD_DIM:(h + 1) * HEAD_DIM], jnp.broadcast_to(qc_ref[h], (tm, HEAD_DIM))],
            axis=1).astype(BF16)
    ones_col = jnp.where(_iota((tm, HEAD_DIM), 1) == 0, 1.0, 0.0)
    off = MIX_WIDTH
    for ref, with_ones in ((kc_ref, False), (vc_ref, False), (vs_ref, True), (vw_ref, True)):
        for g in range(KV_GROUPS):
            t = res[:, off + g * HEAD_DIM: off + (g + 1) * HEAD_DIM]
            if with_ones:
                t = jnp.concatenate([t, ones_col], axis=1)
            ref[0, g] = t.astype(BF16)
        off += KV_WIDTH
    qm_ref[0] = res[:, off:off + MEM_WIDTH].astype(BF16)
    off += MEM_WIDTH
    gates = jax.nn.sigmoid(res[:, off:off + 64] + gb_ref[...])
    for g in range(KV_GROUPS):
        gate_ref[0, g] = gates[:, g * 16:(g + 1) * 16]
    rt = _mm(wt_ref[...], hn, _NT)
    kst_ref[0] = rt[:KV_WIDTH].astype(BF16)
    kwt_ref[0] = rt[KV_WIDTH:].astype(BF16)


def _alibi_slopes(n):
    def pow2(m):
        start = 2.0 ** (-8.0 / m)
        return [start ** (i + 1) for i in range(m)]
    c = 2 ** int(math.floor(math.log2(n)))
    s = pow2(c)
    if c < n:
        s = s + pow2(2 * c)[0::2][: n - c]
    return np.asarray(s, dtype=np.float32)


def _pos_pieces(pos):
    a64 = (pos >> 6) * 64
    b = pos & 63
    return np.stack([a64, a64, a64, b, b, b]).astype(np.float32)


def _slope_pieces():
    sl = jnp.asarray(_alibi_slopes(MIX_HEADS) * np.float32(LOG2E), F32)
    s1 = sl.astype(BF16).astype(F32)
    s2 = (sl - s1).astype(BF16).astype(F32)
    s3 = (sl - s1 - s2).astype(BF16).astype(F32)
    six = jnp.stack([s1, s2, s3, s1, s2, s3], axis=1)
    return jnp.pad(six, ((0, 0), (0, HEAD_DIM - 6))).reshape(MIX_HEADS, 1, HEAD_DIM)


def _nsa_proj(x, norm1, w_in, gate_b):
    B, S, D = x.shape
    TM = min(512, S)
    G, H, Dh = KV_GROUPS, MIX_HEADS, HEAD_DIM
    scale = HEAD_DIM ** -0.5
    o = np.cumsum([0, MIX_WIDTH] + [KV_WIDTH] * 6 + [3 * MIX_HEADS, MEM_WIDTH])
    wq, wkc, wvc, wks, wvs, wkw, wvw, wgl, wqm = (w_in[:, o[i]:o[i + 1]] for i in range(9))
    wgl = wgl.reshape(D, 3, G, GROUP_SIZE).transpose(0, 2, 1, 3).reshape(D, G, 9)
    wgl = jnp.pad(wgl, ((0, 0), (0, 0), (0, 7))).reshape(D, 64)
    gb = gate_b.reshape(3, G, GROUP_SIZE).transpose(1, 0, 2).reshape(G, 9)
    gb = jnp.pad(gb, ((0, 0), (0, 7))).reshape(1, 64)
    w_row = jnp.concatenate(
        [wq * (scale * LOG2E), wkc, wvc, wvs, wvw, wqm * scale, wgl, jnp.zeros((D, LANES - 64), F32)],
        axis=1).astype(BF16)
    w_t = jnp.concatenate([wks, wkw], axis=1).T.astype(BF16)
    head = lambda n, w: pl.BlockSpec((1, n, TM, w), lambda b, s: (b, 0, s, 0))
    return pl.pallas_call(
        _nsa_proj_kernel,
        grid=(B, S // TM),
        in_specs=[
            pl.BlockSpec((1, TM, D), lambda b, s: (b, s, 0)),
            pl.BlockSpec((1, D), lambda b, s: (0, 0)),
            pl.BlockSpec((D, NSA_ROW_W), lambda b, s: (0, 0)),
            pl.BlockSpec((2 * KV_WIDTH, D), lambda b, s: (0, 0)),
            pl.BlockSpec((1, 64), lambda b, s: (0, 0)),
            pl.BlockSpec((H, 1, Dh), lambda b, s: (0, 0, 0)),
        ],
        out_specs=[
            head(H, LANES), head(G, Dh), head(G, Dh), head(G, LANES), head(G, LANES),
            pl.BlockSpec((1, KV_WIDTH, TM), lambda b, s: (b, 0, s)),
            pl.BlockSpec((1, KV_WIDTH, TM), lambda b, s: (b, 0, s)),
            pl.BlockSpec((1, TM, MEM_WIDTH), lambda b, s: (b, s, 0)),
            pl.BlockSpec((1, G, TM, 16), lambda b, s: (b, 0, s, 0)),
        ],
        out_shape=[
            jax.ShapeDtypeStruct((B, H, S, LANES), BF16),
            jax.ShapeDtypeStruct((B, G, S, Dh), BF16),
            jax.ShapeDtypeStruct((B, G, S, Dh), BF16),
            jax.ShapeDtypeStruct((B, G, S, LANES), BF16),
            jax.ShapeDtypeStruct((B, G, S, LANES), BF16),
            jax.ShapeDtypeStruct((B, KV_WIDTH, S), BF16),
            jax.ShapeDtypeStruct((B, KV_WIDTH, S), BF16),
            jax.ShapeDtypeStruct((B, S, MEM_WIDTH), BF16),
            jax.ShapeDtypeStruct((B, G, S, 16), F32),
        ],
        compiler_params=_cparams(("parallel", "parallel")),
        name="nsa_proj",
    )(x, norm1.reshape(1, D), w_row, w_t, gb, _slope_pieces())


def _compress_kernel(kc_ref, vc_ref, w1_ref, w2_ref, pos_ref, cpos_ref, kcb_ref, vcb_ref):
    half = CMP_STRIDE * HEAD_DIM
    for idx, (src, dst) in enumerate(((kc_ref, kcb_ref), (vc_ref, vcb_ref))):
        c = src[0, 0]
        ncp = c.shape[0]
        a = _mm(c, w1_ref[idx, :half])
        bm = _mm(c, w1_ref[idx, half:])
        bias = _mm(pos_ref[idx], w1_ref[idx])[0:1]
        hid = jax.nn.gelu(a + pltpu.roll(bm, ncp - 1, 0) + bias)
        out = _mm(hid, w2_ref[idx])
        out = jnp.where(_iota(out.shape, 0) < ncp - 1, out, 0.0)
        if idx == 0:
            out = jnp.concatenate([out, cpos_ref[...]], axis=1)
        dst[0, 0] = out.astype(BF16)


def _compress(kc, vc, cmp_pos, cmp_w1, cmp_w2):
    B, G, S, Dh = kc.shape
    NCP = S // CMP_STRIDE
    kcr = kc.reshape(B, G, NCP, CMP_STRIDE * Dh)
    vcr = vc.reshape(B, G, NCP, CMP_STRIDE * Dh)
    w1 = cmp_w1.reshape(2, CMP_BLOCK * Dh, CMP_HIDDEN).astype(BF16)
    w2 = cmp_w2.astype(BF16)
    pos = jnp.broadcast_to(cmp_pos.reshape(2, 1, CMP_BLOCK * Dh), (2, 8, CMP_BLOCK * Dh)).astype(BF16)
    cend = np.arange(NCP) * CMP_STRIDE + CMP_BLOCK - 1
    cpos = np.zeros((NCP, Dh), np.float32)
    cpos[:, :6] = _pos_pieces(cend).T
    blk = pl.BlockSpec((1, 1, NCP, CMP_STRIDE * Dh), lambda b, g: (b, g, 0, 0))
    return pl.pallas_call(
        _compress_kernel,
        grid=(B, G),
        in_specs=[
            blk, blk,
            pl.BlockSpec((2, CMP_BLOCK * Dh, CMP_HIDDEN), lambda b, g: (0, 0, 0)),
            pl.BlockSpec((2, CMP_HIDDEN, Dh), lambda b, g: (0, 0, 0)),
            pl.BlockSpec((2, 8, CMP_BLOCK * Dh), lambda b, g: (0, 0, 0)),
            pl.BlockSpec((NCP, Dh), lambda b, g: (0, 0)),
        ],
        out_specs=[pl.BlockSpec((1, 1, NCP, LANES), lambda b, g: (b, g, 0, 0)),
                   pl.BlockSpec((1, 1, NCP, Dh), lambda b, g: (b, g, 0, 0))],
        out_shape=[jax.ShapeDtypeStruct((B, G, NCP, LANES), BF16),
                   jax.ShapeDtypeStruct((B, G, NCP, Dh), BF16)],
        compiler_params=_cparams(("parallel", "parallel")),
        name="nsa_compress",
    )(kcr, vcr, w1, w2, pos, jnp.asarray(cpos))


def _exp2_bf16(s, m):
    return jnp.concatenate(
        [jnp.exp2((s[:, i * LANES:(i + 1) * LANES] - m).astype(BF16)) for i in range(s.shape[1] // LANES)],
        axis=1)


def _nsa_attn_kernel(q_ref, kcb_ref, vcb_ref, ovt_ref, kst_ref, ksc_ref, vs_ref, kwt_ref, kwc_ref,
                     vw_ref, gate_ref, o_ref, score_scr, m_scr, acc_scr, *, TQ, TK, S, n_sel):
    R, Dh = GROUP_SIZE, HEAD_DIM
    t0 = pl.program_id(2) * TQ
    NCP, NB = S // CMP_STRIDE, S // SEL_BLOCK
    qa = q_ref[0].reshape(R * TQ, LANES)
    tq_col = t0 + _iota((TQ, 1), 0)
    tq_col3 = t0 + (_iota((R * TQ, 1), 0) & (TQ - 1))

    sc = _mm(qa, kcb_ref[0, 0], _NT)
    n_row = _iota((1, NCP), 1)
    cend = n_row * CMP_STRIDE + (CMP_BLOCK - 1)
    mask_c = (cend <= tq_col) & (n_row < NCP - 1)
    psum = jnp.zeros((TQ, NCP), F32)
    pcs = []
    for r in range(R):
        s = jnp.where(mask_c, sc[r * TQ:(r + 1) * TQ], NEG_INF)
        m = jnp.max(s, axis=-1, keepdims=True)
        p = jnp.where(mask_c, jnp.exp2(s - m), 0.0)
        l = jnp.sum(p, axis=-1, keepdims=True)
        p = p * jnp.where(l > 0.0, 1.0 / l, 0.0)
        psum = psum + p
        pcs.append(p)
    o_c = _mm(jnp.concatenate(pcs, axis=0), vcb_ref[0, 0])

    WK = WINDOW + TQ
    w0 = pl.multiple_of(jnp.maximum(t0 - WINDOW, 0), LANES)
    kaug_w = jnp.concatenate([kwt_ref[0, :, pl.ds(w0, WK)], kwc_ref[:, pl.ds(w0, WK)]], axis=0)
    sw = jnp.dot(qa, kaug_w, preferred_element_type=F32)
    dist_w = tq_col3 - (w0 + _iota((1, WK), 1))
    sw = jnp.where((dist_w >= 0) & (dist_w < WINDOW), sw, NEG_INF)
    vwin = vw_ref[0, 0, pl.ds(w0, WK), :]
    ows = []
    for r in range(R):
        rows = slice(r * TQ, (r + 1) * TQ)
        m = jnp.broadcast_to(jnp.max(sw[rows], axis=-1, keepdims=True), (TQ, LANES))
        ows.append(jnp.dot(_exp2_bf16(sw[rows], m), vwin, preferred_element_type=F32))
    ow = jnp.concatenate(ows, axis=0)

    imp_t = _mm(ovt_ref[...], psum, _NT, hi=True)
    tb = lax.shift_right_logical(t0 + _iota((1, TQ), 1), 6)
    j_col = _iota((NB, 1), 0)
    valid = j_col <= tb
    forced = (j_col == 0) | (j_col == tb) | (j_col == tb - 1)
    score = jnp.where(valid, jnp.where(forced, FORCE_SCORE, imp_t), -jnp.inf)
    score_scr[...] = score
    n_valid = (t0 + TQ - 1) // SEL_BLOCK + 1

    def rank_body(i, cnt):
        row = score_scr[pl.ds(i, 1), :]
        tie = jnp.where(i < j_col, 1.0, 0.0)
        return cnt + jnp.where(row > score, 1.0, jnp.where(row == score, tie, 0.0))

    rank = lax.fori_loop(0, n_valid, rank_body, jnp.zeros((NB, TQ), F32))
    unsel_t = jnp.where(valid, jnp.where(rank < n_sel, 0.0, 1.0), 1.0)
    unsel = jnp.concatenate([unsel_t, jnp.zeros((LANES - NB, TQ), F32)], axis=0).T.astype(BF16)
    q2 = jnp.concatenate([qa, jnp.concatenate([unsel] * R, axis=0)], axis=1)

    m_scr[...] = jnp.full(m_scr.shape, NEG_INF, F32)
    acc_scr[...] = jnp.zeros(acc_scr.shape, F32)
    n_kt = (t0 + TQ - 1) // TK + 1

    def scores(kt):
        k0 = pl.multiple_of(kt * TK, TK)
        kaug = jnp.concatenate([kst_ref[0, :, pl.ds(k0, TK)], ksc_ref[:, pl.ds(k0, TK)]], axis=0)
        return jnp.dot(q2, kaug, preferred_element_type=F32)

    def accumulate(kt, s, causal):
        k0 = pl.multiple_of(kt * TK, TK)
        if causal:
            s = jnp.where(k0 + _iota((1, TK), 1) <= tq_col3, s, NEG_INF)
        v = vs_ref[0, 0, pl.ds(k0, TK), :]
        for r in range(R):
            rows = slice(r * TQ, (r + 1) * TQ)
            m_old = m_scr[rows]
            m_new = jnp.maximum(m_old, jnp.max(s[rows], axis=-1, keepdims=True))
            p = _exp2_bf16(s[rows], m_new)
            pv = jnp.dot(p, v, preferred_element_type=F32)
            acc_scr[rows] = jnp.exp2(m_old - m_new) * acc_scr[rows] + pv
            m_scr[rows] = m_new

    def sel_body(kt, s):
        s_next = scores(kt + 1)
        accumulate(kt, s, False)
        return s_next

    s_last = lax.fori_loop(0, n_kt - 1, sel_body, scores(0))
    accumulate(n_kt - 1, s_last, True)

    acc = acc_scr[...]
    o_s = acc[:, :Dh] * (1.0 / acc[:, Dh:Dh + 1])
    o_w = ow[:, :Dh] * (1.0 / ow[:, Dh:Dh + 1])
    gv = gate_ref[0, 0]
    for r in range(R):
        rows = slice(r * TQ, (r + 1) * TQ)
        out = (gv[:, r:r + 1] * o_c[rows] + gv[:, R + r:R + r + 1] * o_s[rows]
               + gv[:, 2 * R + r:2 * R + r + 1] * o_w[rows])
        o_ref[0, 0, :, r * Dh:(r + 1) * Dh] = out.astype(BF16)


def _nsa_attn(q, kcb, vcb, kst, vs, kwt, vw, gates):
    B, H, S, _ = q.shape
    G, R, Dh = KV_GROUPS, GROUP_SIZE, HEAD_DIM
    TQ = 128
    TK = min(512, S)
    NCP, NB = S // CMP_STRIDE, S // SEL_BLOCK
    n_sel = min(SEL_TOPN, NB)
    assert S % TK == 0 and S >= WINDOW + TQ and NB <= Dh and S <= 4096
    cs = np.arange(NCP) * CMP_STRIDE
    ss = np.arange(NB) * SEL_BLOCK
    ov = (cs[:, None] <= ss[None, :] + SEL_BLOCK - 1) & (cs[:, None] + CMP_BLOCK - 1 >= ss[None, :])
    ov[NCP - 1] = False
    ovt = jnp.asarray(ov.T.astype(np.float32))
    pieces = _pos_pieces(np.arange(S))
    ksc = np.zeros((3 * Dh, S), np.float32)
    ksc[:6] = pieces
    ksc[Dh:Dh + NB] = np.where(np.arange(S)[None, :] // SEL_BLOCK == np.arange(NB)[:, None], NEG_INF, 0.0)
    kwc = np.zeros((Dh, S), np.float32)
    kwc[:6] = pieces
    kern = functools.partial(_nsa_attn_kernel, TQ=TQ, TK=TK, S=S, n_sel=n_sel)
    return pl.pallas_call(
        kern,
        grid=(B, G, S // TQ),
        in_specs=[
            pl.BlockSpec((1, R, TQ, LANES), lambda b, g, i: (b, g, i, 0)),
            pl.BlockSpec((1, 1, NCP, LANES), lambda b, g, i: (b, g, 0, 0)),
            pl.BlockSpec((1, 1, NCP, Dh), lambda b, g, i: (b, g, 0, 0)),
            pl.BlockSpec((NB, NCP), lambda b, g, i: (0, 0)),
            pl.BlockSpec((1, Dh, S), lambda b, g, i: (b, g, 0)),
            pl.BlockSpec((3 * Dh, S), lambda b, g, i: (0, 0)),
            pl.BlockSpec((1, 1, S, LANES), lambda b, g, i: (b, g, 0, 0)),
            pl.BlockSpec((1, Dh, S), lambda b, g, i: (b, g, 0)),
            pl.BlockSpec((Dh, S), lambda b, g, i: (0, 0)),
            pl.BlockSpec((1, 1, S, LANES), lambda b, g, i: (b, g, 0, 0)),
            pl.BlockSpec((1, 1, TQ, 16), lambda b, g, i: (b, g, i, 0)),
        ],
        out_specs=pl.BlockSpec((1, 1, TQ, R * Dh), lambda b, g, i: (b, g, i, 0)),
        out_shape=jax.ShapeDtypeStruct((B, G, S, R * Dh), BF16),
        scratch_shapes=[
            pltpu.VMEM((NB, TQ), F32),
            pltpu.VMEM((R * TQ, LANES), F32),
            pltpu.VMEM((R * TQ, LANES), F32),
        ],
        compiler_params=_cparams(("parallel", "parallel", "arbitrary")),
        name="nsa_attn",
    )(q, kcb, vcb, ovt, kst, jnp.asarray(ksc, BF16), vs, kwt, jnp.asarray(kwc, BF16), vw, gates)


RW_Z_W = 3 * MIX_WIDTH + 2 * LANES + 2 * LANES
RW_ROW_W = RW_Z_W + MEM_WIDTH


def _rw_prep_kernel(x_ref, g_ref, w_ref, mu_ref, w0_ref, w2_ref, a0_ref, a2_ref, g2_ref, kk_ref,
                    ka_ref, r_ref, lw_ref, kx_ref, km_ref, v_ref, a_ref, go_ref, qm_ref, carry_scr):
    W = MIX_WIDTH

    @pl.when(pl.program_id(1) == 0)
    def _():
        carry_scr[...] = jnp.zeros(carry_scr.shape, F32)

    hn = _rms(x_ref[0], g_ref[...]).astype(BF16)
    res = _mm(hn, w_ref[...])
    qm_ref[0] = res[:, RW_Z_W:].astype(BF16)
    z = res[:, :RW_Z_W]
    tm = z.shape[0]
    zprev = jnp.where(_iota((tm, 1), 0) == 0, carry_scr[0:1, :], pltpu.roll(z, 1, 0))
    carry_scr[0:1, :] = z[tm - 1:tm, :]
    z = z + (zprev - z) * mu_ref[...]
    r, k, v = z[:, :W], z[:, W:2 * W], z[:, 2 * W:3 * W]
    zw = z[:, 3 * W:3 * W + LANES]
    za = z[:, 3 * W + LANES:3 * W + 2 * LANES]
    zg = z[:, 3 * W + 2 * LANES:]
    w_log = -jax.nn.softplus(-(w0_ref[...] + _mm(jnp.tanh(zw), w2_ref[...]))) - 0.5
    a = jax.nn.sigmoid(a0_ref[...] + _mm(za, a2_ref[...]))
    r_ref[0] = r
    lw_ref[0] = -jnp.exp(w_log)
    kx_ref[0] = k * kk_ref[...]
    km_ref[0] = k * (1.0 + (a - 1.0) * ka_ref[...])
    v_ref[0] = v
    a_ref[0] = a
    go_ref[0] = _mm(jax.nn.sigmoid(zg), g2_ref[...])


def _pad_rows(w, n):
    return jnp.pad(w, ((0, n - w.shape[0]), (0, 0)))


def _rw_prep(x, norm1, w_in, mu, w0, w2, a0, a2, g2, k_k, k_a):
    B, S, D = x.shape
    W = MIX_WIDTH
    TM = min(256, S)
    o = np.cumsum([0, 3 * W, DECAY_LORA, ICLR_LORA, GATE_LORA, MEM_WIDTH])
    seg = [w_in[:, o[i]:o[i + 1]] for i in range(5)]
    padc = lambda w, n: jnp.pad(w, ((0, 0), (0, n - w.shape[1])))
    w_row = jnp.concatenate(
        [seg[0], padc(seg[1], LANES), padc(seg[2], LANES), padc(seg[3], 2 * LANES),
         seg[4] * HEAD_DIM ** -0.5], axis=1).astype(BF16)
    mus = [mu[o[i]:o[i + 1]] for i in range(4)]
    padv = lambda v, n: jnp.pad(v, (0, n - v.shape[0]))
    mu_p = jnp.concatenate([mus[0], padv(mus[1], LANES), padv(mus[2], LANES),
                            padv(mus[3], 2 * LANES)]).reshape(1, RW_Z_W)
    vec = lambda v: v.reshape(1, W)
    full = lambda a: pl.BlockSpec(a.shape, lambda b, s: (0,) * a.ndim)
    args = [norm1.reshape(1, D), w_row, mu_p, vec(w0), _pad_rows(w2, LANES).astype(BF16), vec(a0),
            _pad_rows(a2, LANES).astype(BF16), _pad_rows(g2, 2 * LANES).astype(BF16), vec(k_k), vec(k_a)]
    oblk = pl.BlockSpec((1, TM, W), lambda b, s: (b, s, 0))
    return pl.pallas_call(
        _rw_prep_kernel,
        grid=(B, S // TM),
        in_specs=[pl.BlockSpec((1, TM, D), lambda b, s: (b, s, 0))] + [full(a) for a in args],
        out_specs=[oblk] * 7 + [pl.BlockSpec((1, TM, MEM_WIDTH), lambda b, s: (b, s, 0))],
        out_shape=[jax.ShapeDtypeStruct((B, S, W), F32)] * 7
        + [jax.ShapeDtypeStruct((B, S, MEM_WIDTH), BF16)],
        scratch_shapes=[pltpu.VMEM((8, RW_Z_W), F32)],
        compiler_params=_cparams(("parallel", "arbitrary")),
        name="rw_prep",
    )(x, *args)


def _split2(x):
    hi = x.astype(BF16)
    return hi, (x - hi.astype(F32)).astype(BF16)


def _mm3(a, b):
    dot = functools.partial(jnp.dot, preferred_element_type=F32)
    return dot(a[0], b[0]) + dot(a[0], b[1]) + dot(a[1], b[0])


def _each(f, *lists):
    return [f(*args) for args in zip(*lists)]


def _tri_inverse(a2s, eye, same16, same32, same64):
    ds = _each(lambda a2: jnp.where(same16, a2, 0.0), a2s)
    ps = _each(lambda d: eye + d, ds)
    dss = _each(_split2, ds)
    for _ in range(3):
        dss = _each(lambda d: _split2(_mm3(d, d)), dss)
        ps = _each(lambda p, d: p + _mm3(_split2(p), d), ps, dss)
    for lo, hi_ in ((same16, same32), (same32, same64)):
        sel = hi_ & jnp.logical_not(lo)
        offs = _each(lambda a2: _split2(jnp.where(sel, a2, 0.0)), a2s)
        pss = _each(_split2, ps)
        mids = _each(lambda p, o: _split2(_mm3(p, o)), pss, offs)
        ps = _each(lambda p, m, q: p + _mm3(m, q), ps, mids, pss)
    return ps


def _rw_scan_kernel(r_ref, lw_ref, kx_ref, km_ref, v_ref, a_ref, g_ref, rk_ref, lnw_ref, lnb_ref,
                    o_ref, st_scr, *, TS, C):
    Dh = HEAD_DIM

    @pl.when(pl.program_id(1) == 0)
    def _():
        st_scr[...] = jnp.zeros(st_scr.shape, F32)

    head0 = _iota((1, LANES), 1) < Dh
    ltri = jnp.where(_iota((C, C), 1) <= _iota((C, C), 0), 1.0, 0.0)
    col2 = _iota((C, LANES), 1) & (Dh - 1)
    row2 = _iota((C, LANES), 0)
    m_incl = col2 <= row2
    m_strict = col2 < row2
    r128 = _iota((LANES, LANES), 0)
    c128 = _iota((LANES, LANES), 1)
    eye = jnp.where(r128 == c128, 1.0, 0.0)
    same16 = (r128 >> 4) == (c128 >> 4)
    same32 = (r128 >> 5) == (c128 >> 5)
    same64 = (r128 >> 6) == (c128 >> 6)
    zeros = jnp.zeros((C, LANES), F32)
    ltri = ltri.astype(BF16)

    def hsum(x):
        s0 = jnp.sum(jnp.where(head0, x, 0.0), axis=-1, keepdims=True)
        s1 = jnp.sum(jnp.where(head0, 0.0, x), axis=-1, keepdims=True)
        return jnp.where(head0, s0, s1)

    n_pairs = MIX_WIDTH // LANES
    lns = [slice(pi * LANES, (pi + 1) * LANES) for pi in range(n_pairs)]
    pis = list(range(n_pairs))

    def cumsum_decay(lw):
        l1 = lw.astype(BF16)
        l2 = (lw - l1.astype(F32)).astype(BF16)
        l3 = (lw - l1.astype(F32) - l2.astype(F32)).astype(BF16)
        cum3 = jnp.dot(ltri, jnp.concatenate([l1, l2, l3], axis=1), preferred_element_type=F32)
        return cum3[:, :LANES] + cum3[:, LANES:2 * LANES] + cum3[:, 2 * LANES:]

    def scaled(ln, cum, sl):
        r, lw, kx, km, a = r_ref[0, sl, ln], lw_ref[0, sl, ln], kx_ref[0, sl, ln], km_ref[0, sl, ln], a_ref[0, sl, ln]
        kk = kx / jnp.maximum(jnp.sqrt(hsum(kx * kx)), 1e-12)
        p_in = jnp.exp(cum)
        at = -kk * jnp.exp(cum - lw)
        rt = r * p_in
        p_inv = jnp.exp(-cum)
        bk = jnp.concatenate([kk * a * p_inv, km * p_inv], axis=0)
        at0, at1 = jnp.where(head0, at, 0.0), jnp.where(head0, 0.0, at)
        rt0, rt1 = jnp.where(head0, rt, 0.0), jnp.where(head0, 0.0, rt)
        lhs = jnp.concatenate([at0, at1, rt0, rt1], axis=0)
        return lhs, bk, rt, p_in[C - 1:C, :]

    def split_aa(aa):
        aa0 = jnp.where(m_strict, aa[0:C], 0.0)
        aa1 = pltpu.roll(jnp.where(m_strict, aa[C:2 * C], 0.0), Dh, 1)
        ar0 = jnp.where(m_incl, aa[2 * C:3 * C], 0.0)
        ar1 = jnp.where(m_incl, aa[3 * C:4 * C], 0.0)
        a2 = jnp.concatenate([jnp.where(head0, aa0, 0.0), jnp.where(head0, 0.0, aa1)], axis=0)
        return aa0, aa1, ar0, ar1, a2

    def epilogue(ln, y, sl):
        r, km, v = r_ref[0, sl, ln], km_ref[0, sl, ln], v_ref[0, sl, ln]
        mean = hsum(y) * (1.0 / Dh)
        d = y - mean
        var = hsum(d * d) * (1.0 / Dh)
        yn = d * lax.rsqrt(var + GN_EPS) * lnw_ref[:, ln] + lnb_ref[:, ln]
        bonus = hsum(r * km * rk_ref[:, ln]) * v
        o_ref[0, sl, ln] = ((yn + bonus) * g_ref[0, sl, ln]).astype(BF16)

    def chunk(c, carry):
        sl = pl.ds(pl.multiple_of(c * C, C), C)
        cums = _each(lambda ln: cumsum_decay(lw_ref[0, sl, ln]), lns)
        lhss, bks, rts, pcs = zip(*_each(lambda ln, cum: scaled(ln, cum, sl), lns, cums))
        aas = _each(lambda lhs, bk: _mm(lhs, bk, _NT), lhss, bks)
        aa0s, aa1s, ar0s, ar1s, a2s = zip(*_each(split_aa, aas))
        t2s = _tri_inverse(a2s, eye, same16, same32, same64)
        vs = _each(lambda ln: v_ref[0, sl, ln], lns)
        x0s = _each(lambda aa0, v: _mm(aa0, jnp.concatenate([zeros, v], axis=0)), aa0s, vs)
        x1s = _each(lambda aa1, v: _mm(aa1, jnp.concatenate([v, zeros], axis=0)), aa1s, vs)
        wus = _each(lambda t2, lhs, x0, x1: _mm(t2, jnp.concatenate(
            [lhs[:2 * C], jnp.concatenate([x0, x1], axis=0)], axis=1)), t2s, lhss, x0s, x1s)
        sts = _each(lambda pi: st_scr[pi], pis)
        us = _each(lambda wu, st: _mm(wu[:C, :LANES] + wu[C:, :LANES], st, _NT)
                   + jnp.where(head0, wu[:C, LANES:], wu[C:, LANES:]), wus, sts)
        uvs = _each(lambda u, v: jnp.concatenate([u, v], axis=0), us, vs)
        ys = _each(lambda rt, st, ar0, ar1, uv: _mm(rt, st, _NT)
                   + jnp.where(head0, _mm(ar0, uv), _mm(ar1, uv)), rts, sts, ar0s, ar1s, uvs)
        new = _each(lambda st, pc, uv, bk: st * pc + jnp.where(same64, _mm(uv, bk * pc, _TN), 0.0),
                    sts, pcs, uvs, bks)
        for pi in pis:
            st_scr[pi] = new[pi]
        _each(lambda ln, y: epilogue(ln, y, sl), lns, ys)
        return carry

    lax.fori_loop(0, TS // C, chunk, 0)


def _rw_scan(r, lw, kx, km, v, a, g, r_k, lnx_w, lnx_b):
    B, S, W = r.shape
    TS = min(256, S)
    C = 64
    blk = pl.BlockSpec((1, TS, W), lambda b, s: (b, s, 0))
    vblk = pl.BlockSpec((1, W), lambda b, s: (0, 0))
    kern = functools.partial(_rw_scan_kernel, TS=TS, C=C)
    return pl.pallas_call(
        kern,
        grid=(B, S // TS),
        in_specs=[blk] * 7 + [vblk] * 3,
        out_specs=blk,
        out_shape=jax.ShapeDtypeStruct((B, S, W), BF16),
        scratch_shapes=[pltpu.VMEM((W // LANES, LANES, LANES), F32)],
        compiler_params=_cparams(("parallel", "arbitrary")),
        name="rw_scan",
    )(r, lw, kx, km, v, a, g, r_k.reshape(1, W), lnx_w.reshape(1, W), lnx_b.reshape(1, W))


def kernel(x, mem, norm1, norm_mem, w_mem_kv, w_o, norm2, w_ffn_in, w_ffn_out, nsa_w_in, nsa_gate_b,
           nsa_cmp_pos, nsa_cmp_w1, nsa_cmp_w2, rw_w_in, rw_mu, rw_w0, rw_w2, rw_a0, rw_a2, rw_g2,
           rw_k_k, rw_k_a, rw_r_k, rw_lnx_w, rw_lnx_b, final_norm):
    depth = norm1.shape[0]
    B, S, _ = x.shape
    ktm, vm = _mem_kv(mem, norm_mem, w_mem_kv)
    for i in range(depth):
        j = i // 2
        if i % 2 == 0:
            q, kc, vc, vs, vw, kst, kwt, qm, gates = _nsa_proj(x, norm1[i], nsa_w_in[j], nsa_gate_b[j])
            kcb, vcb = _compress(kc, vc, nsa_cmp_pos[j], nsa_cmp_w1[j], nsa_cmp_w2[j])
            mix = _nsa_attn(q, kcb, vcb, kst, vs, kwt, vw, gates)
        else:
            r, lw, kx, km, v, a, g, qm = _rw_prep(x, norm1[i], rw_w_in[j], rw_mu[j], rw_w0[j], rw_w2[j],
                                                  rw_a0[j], rw_a2[j], rw_g2[j], rw_k_k[j], rw_k_a[j])
            mix = _rw_scan(r, lw, kx, km, v, a, g, rw_r_k[j], rw_lnx_w[j], rw_lnx_b[j])
            mix = mix.reshape(B, 1, S, MIX_WIDTH)
        cross = _mem_attn(qm, ktm[i], vm[i])
        x = _post(x, mix, cross, w_o[i], norm2[i], w_ffn_in[i], w_ffn_out[i], final_norm,
                  final=(i == depth - 1))
    return x
```

```python
import functools
import math

import numpy as np
import jax
import jax.numpy as jnp
from jax import lax
from jax.experimental import pallas as pl
from jax.experimental.pallas import tpu as pltpu

F32 = jnp.float32
BF16 = jnp.bfloat16
HI = lax.Precision.HIGHEST

D_MODEL = 1024
HEAD_DIM = 64
MIX_WIDTH = 768
MIX_HEADS = 12
MEM_HEADS = 4
MEM_WIDTH = 256
KV_GROUPS = 4
GROUP_SIZE = 3
KV_WIDTH = 256
CMP_BLOCK = 32
CMP_STRIDE = 16
CMP_HIDDEN = 128
SEL_BLOCK = 64
SEL_TOPN = 16
WINDOW = 512
FORCE_SCORE = 1.0e4
DECAY_LORA = 64
ICLR_LORA = 64
GATE_LORA = 160
GN_EPS = 64e-5
FFN_HIDDEN = 2816
RMS_EPS = 1e-6
NEG_INF = -1e30
LOG2E = 1.4426950408889634

LANES = 128
VMEM_LIMIT = 56 * 1024 * 1024

_NT = (((1,), (1,)), ((), ()))
_TN = (((0,), (0,)), ((), ()))


def _mm(a, b, dims=None, hi=False):
    if hi:
        a, b, prec = a.astype(F32), b.astype(F32), HI
    else:
        a, b, prec = a.astype(BF16), b.astype(BF16), None
    if dims is None:
        return jnp.dot(a, b, preferred_element_type=F32, precision=prec)
    return lax.dot_general(a, b, dims, preferred_element_type=F32, precision=prec)


def _iota(shape, dim):
    return lax.broadcasted_iota(jnp.int32, shape, dim)


def _rms(x, g):
    ms = jnp.mean(x * x, axis=-1, keepdims=True)
    return x * lax.rsqrt(ms + RMS_EPS) * g


def _cparams(sem):
    return pltpu.CompilerParams(dimension_semantics=sem, vmem_limit_bytes=VMEM_LIMIT)


def _mem_kv_kernel(mem_ref, g_ref, wkt_ref, wv_ref, kt_ref, v_ref):
    mn = _rms(mem_ref[0], g_ref[0]).astype(BF16)
    kt = _mm(wkt_ref[0], mn, _NT)
    v = _mm(mn, wv_ref[0])
    rowh = _iota(kt.shape, 0) // HEAD_DIM
    colh = _iota(v.shape, 1) // HEAD_DIM
    for h in range(MEM_HEADS):
        kt_ref[0, 0, h] = jnp.where(rowh == h, kt, 0.0).astype(BF16)
        v_ref[0, 0, h] = jnp.where(colh == h, v, 0.0).astype(BF16)


def _mem_kv(mem, norm_mem, w_mem_kv):
    B, M, D = mem.shape
    L = norm_mem.shape[0]
    wkt = jnp.swapaxes(w_mem_kv[:, :, :MEM_WIDTH], 1, 2).astype(BF16)
    wv = w_mem_kv[:, :, MEM_WIDTH:].astype(BF16)
    return pl.pallas_call(
        _mem_kv_kernel,
        grid=(L, B),
        in_specs=[
            pl.BlockSpec((1, M, D), lambda l, b: (b, 0, 0)),
            pl.BlockSpec((1, 1, D), lambda l, b: (l, 0, 0)),
            pl.BlockSpec((1, MEM_WIDTH, D), lambda l, b: (l, 0, 0)),
            pl.BlockSpec((1, D, MEM_WIDTH), lambda l, b: (l, 0, 0)),
        ],
        out_specs=[
            pl.BlockSpec((1, 1, MEM_HEADS, MEM_WIDTH, M), lambda l, b: (l, b, 0, 0, 0)),
            pl.BlockSpec((1, 1, MEM_HEADS, M, MEM_WIDTH), lambda l, b: (l, b, 0, 0, 0)),
        ],
        out_shape=[
            jax.ShapeDtypeStruct((L, B, MEM_HEADS, MEM_WIDTH, M), BF16),
            jax.ShapeDtypeStruct((L, B, MEM_HEADS, M, MEM_WIDTH), BF16),
        ],
        compiler_params=_cparams(("parallel", "parallel")),
        name="mem_kv",
    )(mem, norm_mem.reshape(L, 1, D), wkt, wv)


def _mem_attn_kernel(q_ref, kt_ref, v_ref, o_ref):
    q = q_ref[0]
    acc = jnp.zeros(q.shape, F32)
    for h in range(MEM_HEADS):
        s = _mm(q, kt_ref[0, h])
        m = jnp.max(s, axis=-1, keepdims=True)
        p = jnp.exp(s - m)
        l = jnp.sum(p, axis=-1, keepdims=True)
        acc = acc + _mm(p * (1.0 / l), v_ref[0, h])
    o_ref[0] = acc.astype(BF16)


def _mem_attn(qm, ktm, vm):
    B, S, _ = qm.shape
    M = ktm.shape[-1]
    TM = min(512, S)
    return pl.pallas_call(
        _mem_attn_kernel,
        grid=(B, S // TM),
        in_specs=[
            pl.BlockSpec((1, TM, MEM_WIDTH), lambda b, s: (b, s, 0)),
            pl.BlockSpec((1, MEM_HEADS, MEM_WIDTH, M), lambda b, s: (b, 0, 0, 0)),
            pl.BlockSpec((1, MEM_HEADS, M, MEM_WIDTH), lambda b, s: (b, 0, 0, 0)),
        ],
        out_specs=pl.BlockSpec((1, TM, MEM_WIDTH), lambda b, s: (b, s, 0)),
        out_shape=jax.ShapeDtypeStruct((B, S, MEM_WIDTH), BF16),
        compiler_params=_cparams(("parallel", "parallel")),
        name="mem_attn",
    )(qm, ktm, vm)


def _post_kernel(x_ref, mix_ref, cross_ref, wom_ref, woc_ref, g2_ref, wg_ref, wu_ref, wout_ref,
                 gf_ref, o_ref, x1_scr, hn_scr, acc_scr, *, n_parts, n_h, final):
    h = pl.program_id(2)

    @pl.when(h == 0)
    def _():
        x1 = x_ref[0] + _mm(cross_ref[0], woc_ref[...])
        for p in range(n_parts):
            x1 = x1 + _mm(mix_ref[0, p], wom_ref[p])
        x1_scr[...] = x1
        hn_scr[...] = _rms(x1, g2_ref[...]).astype(BF16)
        acc_scr[...] = jnp.zeros(acc_scr.shape, F32)

    hn = hn_scr[...]
    gate = _mm(hn, wg_ref[...])
    up = _mm(hn, wu_ref[...])
    hid = gate * jax.nn.sigmoid(gate) * up
    acc_scr[...] += _mm(hid, wout_ref[...])

    @pl.when(h == n_h - 1)
    def _():
        y = x1_scr[...] + acc_scr[...]
        if final:
            y = _rms(y, gf_ref[...])
        o_ref[0] = y


def _post(x, mix, cross, w_o, norm2, w_ffn_in, w_ffn_out, final_norm, final):
    B, S, D = x.shape
    P, W = mix.shape[1], mix.shape[3]
    TM = min(1024, S)
    TH = 256
    NH = FFN_HIDDEN // TH
    wom = w_o[:MIX_WIDTH].reshape(P, W, D).astype(BF16)
    woc = w_o[MIX_WIDTH:].astype(BF16)
    wi = w_ffn_in.astype(BF16)
    wo = w_ffn_out.astype(BF16)
    kern = functools.partial(_post_kernel, n_parts=P, n_h=NH, final=final)
    return pl.pallas_call(
        kern,
        grid=(B, S // TM, NH),
        in_specs=[
            pl.BlockSpec((1, TM, D), lambda b, s, h: (b, s, 0)),
            pl.BlockSpec((1, P, TM, W), lambda b, s, h: (b, 0, s, 0)),
            pl.BlockSpec((1, TM, MEM_WIDTH), lambda b, s, h: (b, s, 0)),
            pl.BlockSpec((P, W, D), lambda b, s, h: (0, 0, 0)),
            pl.BlockSpec((MEM_WIDTH, D), lambda b, s, h: (0, 0)),
            pl.BlockSpec((1, D), lambda b, s, h: (0, 0)),
            pl.BlockSpec((D, TH), lambda b, s, h: (0, h)),
            pl.BlockSpec((D, TH), lambda b, s, h: (0, NH + h)),
            pl.BlockSpec((TH, D), lambda b, s, h: (h, 0)),
            pl.BlockSpec((1, D), lambda b, s, h: (0, 0)),
        ],
        out_specs=pl.BlockSpec((1, TM, D), lambda b, s, h: (b, s, 0)),
        out_shape=jax.ShapeDtypeStruct((B, S, D), F32),
        scratch_shapes=[
            pltpu.VMEM((TM, D), F32),
            pltpu.VMEM((TM, D), BF16),
            pltpu.VMEM((TM, D), F32),
        ],
        compiler_params=_cparams(("parallel", "parallel", "arbitrary")),
        name="post_ffn",
    )(x, mix, cross, wom, woc, norm2.reshape(1, D), wi, wi, wo, final_norm.reshape(1, D))


NSA_ROW_W = MIX_WIDTH + 4 * KV_WIDTH + MEM_WIDTH + LANES


def _nsa_proj_kernel(x_ref, g_ref, w_ref, wt_ref, gb_ref, qc_ref, q_ref, kc_ref, vc_ref, vs_ref, vw_ref,
                     kst_ref, kwt_ref, qm_ref, gate_ref):
    hn = _rms(x_ref[0], g_ref[...]).astype(BF16)
    res = _mm(hn, w_ref[...])
    tm = res.shape[0]
    for h in range(MIX_HEADS):
        q_ref[0, h] = jnp.concatenate(
            [res[:, h * HEAD_DIM:(h + 1) * HEAD_DIM], jnp.broadcast_to(qc_ref[h], (tm, HEAD_DIM))],
            axis=1).astype(BF16)
    ones_col = jnp.where(_iota((tm, HEAD_DIM), 1) == 0, 1.0, 0.0)
    off = MIX_WIDTH
    for ref, with_ones in ((kc_ref, False), (vc_ref, False), (vs_ref, True), (vw_ref, True)):
        for g in range(KV_GROUPS):
            t = res[:, off + g * HEAD_DIM: off + (g + 1) * HEAD_DIM]
            if with_ones:
                t = jnp.concatenate([t, ones_col], axis=1)
            ref[0, g] = t.astype(BF16)
        off += KV_WIDTH
    qm_ref[0] = res[:, off:off + MEM_WIDTH].astype(BF16)
    off += MEM_WIDTH
    gates = jax.nn.sigmoid(res[:, off:off + 64] + gb_ref[...])
    for g in range(KV_GROUPS):
        gate_ref[0, g] = gates[:, g * 16:(g + 1) * 16]
    rt = _mm(wt_ref[...], hn, _NT)
    kst_ref[0] = rt[:KV_WIDTH].astype(BF16)
    kwt_ref[0] = rt[KV_WIDTH:].astype(BF16)


def _alibi_slopes(n):
    def pow2(m):
        start = 2.0 ** (-8.0 / m)
        return [start ** (i + 1) for i in range(m)]
    c = 2 ** int(math.floor(math.log2(n)))
    s = pow2(c)
    if c < n:
        s = s + pow2(2 * c)[0::2][: n - c]
    return np.asarray(s, dtype=np.float32)


def _pos_pieces(pos):
    a64 = (pos >> 6) * 64
    b = pos & 63
    return np.stack([a64, a64, a64, b, b, b]).astype(np.float32)


def _slope_pieces():
    sl = jnp.asarray(_alibi_slopes(MIX_HEADS) * np.float32(LOG2E), F32)
    s1 = sl.astype(BF16).astype(F32)
    s2 = (sl - s1).astype(BF16).astype(F32)
    s3 = (sl - s1 - s2).astype(BF16).astype(F32)
    six = jnp.stack([s1, s2, s3, s1, s2, s3], axis=1)
    return jnp.pad(six, ((0, 0), (0, HEAD_DIM - 6))).reshape(MIX_HEADS, 1, HEAD_DIM)


def _nsa_proj(x, norm1, w_in, gate_b):
    B, S, D = x.shape
    TM = min(512, S)
    G, H, Dh = KV_GROUPS, MIX_HEADS, HEAD_DIM
    scale = HEAD_DIM ** -0.5
    o = np.cumsum([0, MIX_WIDTH] + [KV_WIDTH] * 6 + [3 * MIX_HEADS, MEM_WIDTH])
    wq, wkc, wvc, wks, wvs, wkw, wvw, wgl, wqm = (w_in[:, o[i]:o[i + 1]] for i in range(9))
    wgl = wgl.reshape(D, 3, G, GROUP_SIZE).transpose(0, 2, 1, 3).reshape(D, G, 9)
    wgl = jnp.pad(wgl, ((0, 0), (0, 0), (0, 7))).reshape(D, 64)
    gb = gate_b.reshape(3, G, GROUP_SIZE).transpose(1, 0, 2).reshape(G, 9)
    gb = jnp.pad(gb, ((0, 0), (0, 7))).reshape(1, 64)
    w_row = jnp.concatenate(
        [wq * (scale * LOG2E), wkc, wvc, wvs, wvw, wqm * scale, wgl, jnp.zeros((D, LANES - 64), F32)],
        axis=1).astype(BF16)
    w_t = jnp.concatenate([wks, wkw], axis=1).T.astype(BF16)
    head = lambda n, w: pl.BlockSpec((1, n, TM, w), lambda b, s: (b, 0, s, 0))
    return pl.pallas_call(
        _nsa_proj_kernel,
        grid=(B, S // TM),
        in_specs=[
            pl.BlockSpec((1, TM, D), lambda b, s: (b, s, 0)),
            pl.BlockSpec((1, D), lambda b, s: (0, 0)),
            pl.BlockSpec((D, NSA_ROW_W), lambda b, s: (0, 0)),
            pl.BlockSpec((2 * KV_WIDTH, D), lambda b, s: (0, 0)),
            pl.BlockSpec((1, 64), lambda b, s: (0, 0)),
            pl.BlockSpec((H, 1, Dh), lambda b, s: (0, 0, 0)),
        ],
        out_specs=[
            head(H, LANES), head(G, Dh), head(G, Dh), head(G, LANES), head(G, LANES),
            pl.BlockSpec((1, KV_WIDTH, TM), lambda b, s: (b, 0, s)),
            pl.BlockSpec((1, KV_WIDTH, TM), lambda b, s: (b, 0, s)),
            pl.BlockSpec((1, TM, MEM_WIDTH), lambda b, s: (b, s, 0)),
            pl.BlockSpec((1, G, TM, 16), lambda b, s: (b, 0, s, 0)),
        ],
        out_shape=[
            jax.ShapeDtypeStruct((B, H, S, LANES), BF16),
            jax.ShapeDtypeStruct((B, G, S, Dh), BF16),
            jax.ShapeDtypeStruct((B, G, S, Dh), BF16),
            jax.ShapeDtypeStruct((B, G, S, LANES), BF16),
            jax.ShapeDtypeStruct((B, G, S, LANES), BF16),
            jax.ShapeDtypeStruct((B, KV_WIDTH, S), BF16),
            jax.ShapeDtypeStruct((B, KV_WIDTH, S), BF16),
            jax.ShapeDtypeStruct((B, S, MEM_WIDTH), BF16),
            jax.ShapeDtypeStruct((B, G, S, 16), F32),
        ],
        compiler_params=_cparams(("parallel", "parallel")),
        name="nsa_proj",
    )(x, norm1.reshape(1, D), w_row, w_t, gb, _slope_pieces())


def _compress_kernel(kc_ref, vc_ref, w1_ref, w2_ref, pos_ref, cpos_ref, kcb_ref, vcb_ref):
    half = CMP_STRIDE * HEAD_DIM
    for idx, (src, dst) in enumerate(((kc_ref, kcb_ref), (vc_ref, vcb_ref))):
        c = src[0, 0]
        ncp = c.shape[0]
        a = _mm(c, w1_ref[idx, :half])
        bm = _mm(c, w1_ref[idx, half:])
        bias = _mm(pos_ref[idx], w1_ref[idx])[0:1]
        hid = jax.nn.gelu(a + pltpu.roll(bm, ncp - 1, 0) + bias)
        out = _mm(hid, w2_ref[idx])
        out = jnp.where(_iota(out.shape, 0) < ncp - 1, out, 0.0)
        if idx == 0:
            out = jnp.concatenate([out, cpos_ref[...]], axis=1)
        dst[0, 0] = out.astype(BF16)


def _compress(kc, vc, cmp_pos, cmp_w1, cmp_w2):
    B, G, S, Dh = kc.shape
    NCP = S // CMP_STRIDE
    kcr = kc.reshape(B, G, NCP, CMP_STRIDE * Dh)
    vcr = vc.reshape(B, G, NCP, CMP_STRIDE * Dh)
    w1 = cmp_w1.reshape(2, CMP_BLOCK * Dh, CMP_HIDDEN).astype(BF16)
    w2 = cmp_w2.astype(BF16)
    pos = jnp.broadcast_to(cmp_pos.reshape(2, 1, CMP_BLOCK * Dh), (2, 8, CMP_BLOCK * Dh)).astype(BF16)
    cend = np.arange(NCP) * CMP_STRIDE + CMP_BLOCK - 1
    cpos = np.zeros((NCP, Dh), np.float32)
    cpos[:, :6] = _pos_pieces(cend).T
    blk = pl.BlockSpec((1, 1, NCP, CMP_STRIDE * Dh), lambda b, g: (b, g, 0, 0))
    return pl.pallas_call(
        _compress_kernel,
        grid=(B, G),
        in_specs=[
            blk, blk,
            pl.BlockSpec((2, CMP_BLOCK * Dh, CMP_HIDDEN), lambda b, g: (0, 0, 0)),
            pl.BlockSpec((2, CMP_HIDDEN, Dh), lambda b, g: (0, 0, 0)),
            pl.BlockSpec((2, 8, CMP_BLOCK * Dh), lambda b, g: (0, 0, 0)),
            pl.BlockSpec((NCP, Dh), lambda b, g: (0, 0)),
        ],
        out_specs=[pl.BlockSpec((1, 1, NCP, LANES), lambda b, g: (b, g, 0, 0)),
                   pl.BlockSpec((1, 1, NCP, Dh), lambda b, g: (b, g, 0, 0))],
        out_shape=[jax.ShapeDtypeStruct((B, G, NCP, LANES), BF16),
                   jax.ShapeDtypeStruct((B, G, NCP, Dh), BF16)],
        compiler_params=_cparams(("parallel", "parallel")),
        name="nsa_compress",
    )(kcr, vcr, w1, w2, pos, jnp.asarray(cpos))


def _exp2_bf16(s, m):
    return jnp.concatenate(
        [jnp.exp2((s[:, i * LANES:(i + 1) * LANES] - m).astype(BF16)) for i in range(s.shape[1] // LANES)],
        axis=1)


def _nsa_attn_kernel(q_ref, kcb_ref, vcb_ref, ovt_ref, kst_ref, ksc_ref, vs_ref, kwt_ref, kwc_ref,
                     vw_ref, gate_ref, o_ref, score_scr, m_scr, acc_scr, *, TQ, TK, S, n_sel):
    R, Dh, CH = GROUP_SIZE, HEAD_DIM, LANES
    t0 = pl.program_id(2) * TQ
    NCP, NB = S // CMP_STRIDE, S // SEL_BLOCK
    qa = q_ref[0].reshape(R * TQ, LANES)
    tq_col = t0 + _iota((TQ, 1), 0)
    tq_col3 = t0 + (_iota((R * TQ, 1), 0) & (TQ - 1))

    sc = _mm(qa, kcb_ref[0, 0], _NT)
    n_row = _iota((1, NCP), 1)
    cend = n_row * CMP_STRIDE + (CMP_BLOCK - 1)
    mask_c = (cend <= tq_col) & (n_row < NCP - 1)
    n_sub = TQ // CH
    psums = [jnp.zeros((CH, NCP), F32) for _ in range(n_sub)]
    pcs = []
    for c in range(R * n_sub):
        mk = mask_c[(c % n_sub) * CH:(c % n_sub + 1) * CH]
        s = jnp.where(mk, sc[c * CH:(c + 1) * CH], NEG_INF)
        m = jnp.max(s, axis=-1, keepdims=True)
        p = jnp.where(mk, jnp.exp2(s - m), 0.0)
        l = jnp.sum(p, axis=-1, keepdims=True)
        p = p * jnp.where(l > 0.0, 1.0 / l, 0.0)
        psums[c % n_sub] = psums[c % n_sub] + p
        pcs.append(p)
    psum = jnp.concatenate(psums, axis=0)
    o_c = _mm(jnp.concatenate(pcs, axis=0), vcb_ref[0, 0])

    WK = WINDOW + TQ
    w0 = pl.multiple_of(jnp.maximum(t0 - WINDOW, 0), LANES)
    kaug_w = jnp.concatenate([kwt_ref[0, :, pl.ds(w0, WK)], kwc_ref[:, pl.ds(w0, WK)]], axis=0)
    sw = jnp.dot(qa, kaug_w, preferred_element_type=F32)
    dist_w = tq_col3 - (w0 + _iota((1, WK), 1))
    sw = jnp.where((dist_w >= 0) & (dist_w < WINDOW), sw, NEG_INF)
    vwin = vw_ref[0, 0, pl.ds(w0, WK), :]
    ows = []
    for c in range(R * n_sub):
        rows = slice(c * CH, (c + 1) * CH)
        m = jnp.broadcast_to(jnp.max(sw[rows], axis=-1, keepdims=True), (CH, LANES))
        ows.append(jnp.dot(_exp2_bf16(sw[rows], m), vwin, preferred_element_type=F32))
    ow = jnp.concatenate(ows, axis=0)

    imp_t = _mm(ovt_ref[...], psum, _NT, hi=True)
    tb = lax.shift_right_logical(t0 + _iota((1, TQ), 1), 6)
    j_col = _iota((NB, 1), 0)
    valid = j_col <= tb
    forced = (j_col == 0) | (j_col == tb) | (j_col == tb - 1)
    score = jnp.where(valid, jnp.where(forced, FORCE_SCORE, imp_t), -jnp.inf)
    score_scr[...] = score
    n_valid = (t0 + TQ - 1) // SEL_BLOCK + 1

    def rank_body(i, cnt):
        row = score_scr[pl.ds(i, 1), :]
        tie = jnp.where(i < j_col, 1.0, 0.0)
        return cnt + jnp.where(row > score, 1.0, jnp.where(row == score, tie, 0.0))

    rank = lax.fori_loop(0, n_valid, rank_body, jnp.zeros((NB, TQ), F32))
    unsel_t = jnp.where(valid, jnp.where(rank < n_sel, 0.0, 1.0), 1.0)
    unsel = jnp.concatenate([unsel_t, jnp.zeros((LANES - NB, TQ), F32)], axis=0).T.astype(BF16)
    q2 = jnp.concatenate([qa, jnp.concatenate([unsel] * R, axis=0)], axis=1)

    m_scr[...] = jnp.full(m_scr.shape, NEG_INF, F32)
    acc_scr[...] = jnp.zeros(acc_scr.shape, F32)
    n_kt = (t0 + TQ - 1) // TK + 1

    def scores(kt):
        k0 = pl.multiple_of(kt * TK, TK)
        kaug = jnp.concatenate([kst_ref[0, :, pl.ds(k0, TK)], ksc_ref[:, pl.ds(k0, TK)]], axis=0)
        return jnp.dot(q2, kaug, preferred_element_type=F32)

    def accumulate(kt, s, causal):
        k0 = pl.multiple_of(kt * TK, TK)
        if causal:
            s = jnp.where(k0 + _iota((1, TK), 1) <= tq_col3, s, NEG_INF)
        v = vs_ref[0, 0, pl.ds(k0, TK), :]
        for c in range(R * n_sub):
            rows = slice(c * CH, (c + 1) * CH)
            m_old = m_scr[rows]
            m_new = jnp.maximum(m_old, jnp.max(s[rows], axis=-1, keepdims=True))
            p = _exp2_bf16(s[rows], m_new)
            pv = jnp.dot(p, v, preferred_element_type=F32)
            acc_scr[rows] = jnp.exp2(m_old - m_new) * acc_scr[rows] + pv
            m_scr[rows] = m_new

    def sel_body(kt, s):
        s_next = scores(kt + 1)
        accumulate(kt, s, False)
        return s_next

    s_last = lax.fori_loop(0, n_kt - 1, sel_body, scores(0))
    accumulate(n_kt - 1, s_last, True)

    acc = acc_scr[...]
    o_s = acc[:, :Dh] * (1.0 / acc[:, Dh:Dh + 1])
    o_w = ow[:, :Dh] * (1.0 / ow[:, Dh:Dh + 1])
    gv = gate_ref[0, 0]
    for r in range(R):
        rows = slice(r * TQ, (r + 1) * TQ)
        out = (gv[:, r:r + 1] * o_c[rows] + gv[:, R + r:R + r + 1] * o_s[rows]
               + gv[:, 2 * R + r:2 * R + r + 1] * o_w[rows])
        o_ref[0, 0, :, r * Dh:(r + 1) * Dh] = out.astype(BF16)


def _nsa_attn(q, kcb, vcb, kst, vs, kwt, vw, gates):
    B, H, S, _ = q.shape
    G, R, Dh = KV_GROUPS, GROUP_SIZE, HEAD_DIM
    TQ = 256
    TK = min(512, S)
    NCP, NB = S // CMP_STRIDE, S // SEL_BLOCK
    n_sel = min(SEL_TOPN, NB)
    assert S % TK == 0 and S >= WINDOW + TQ and NB <= Dh and S <= 4096
    cs = np.arange(NCP) * CMP_STRIDE
    ss = np.arange(NB) * SEL_BLOCK
    ov = (cs[:, None] <= ss[None, :] + SEL_BLOCK - 1) & (cs[:, None] + CMP_BLOCK - 1 >= ss[None, :])
    ov[NCP - 1] = False
    ovt = jnp.asarray(ov.T.astype(np.float32))
    pieces = _pos_pieces(np.arange(S))
    ksc = np.zeros((3 * Dh, S), np.float32)
    ksc[:6] = pieces
    ksc[Dh:Dh + NB] = np.where(np.arange(S)[None, :] // SEL_BLOCK == np.arange(NB)[:, None], NEG_INF, 0.0)
    kwc = np.zeros((Dh, S), np.float32)
    kwc[:6] = pieces
    kern = functools.partial(_nsa_attn_kernel, TQ=TQ, TK=TK, S=S, n_sel=n_sel)
    return pl.pallas_call(
        kern,
        grid=(B, G, S // TQ),
        in_specs=[
            pl.BlockSpec((1, R, TQ, LANES), lambda b, g, i: (b, g, i, 0)),
            pl.BlockSpec((1, 1, NCP, LANES), lambda b, g, i: (b, g, 0, 0)),
            pl.BlockSpec((1, 1, NCP, Dh), lambda b, g, i: (b, g, 0, 0)),
            pl.BlockSpec((NB, NCP), lambda b, g, i: (0, 0)),
            pl.BlockSpec((1, Dh, S), lambda b, g, i: (b, g, 0)),
            pl.BlockSpec((3 * Dh, S), lambda b, g, i: (0, 0)),
            pl.BlockSpec((1, 1, S, LANES), lambda b, g, i: (b, g, 0, 0)),
            pl.BlockSpec((1, Dh, S), lambda b, g, i: (b, g, 0)),
            pl.BlockSpec((Dh, S), lambda b, g, i: (0, 0)),
            pl.BlockSpec((1, 1, S, LANES), lambda b, g, i: (b, g, 0, 0)),
            pl.BlockSpec((1, 1, TQ, 16), lambda b, g, i: (b, g, i, 0)),
        ],
        out_specs=pl.BlockSpec((1, 1, TQ, R * Dh), lambda b, g, i: (b, g, i, 0)),
        out_shape=jax.ShapeDtypeStruct((B, G, S, R * Dh), BF16),
        scratch_shapes=[
            pltpu.VMEM((NB, TQ), F32),
            pltpu.VMEM((R * TQ, LANES), F32),
            pltpu.VMEM((R * TQ, LANES), F32),
        ],
        compiler_params=_cparams(("parallel", "parallel", "arbitrary")),
        name="nsa_attn",
    )(q, kcb, vcb, ovt, kst, jnp.asarray(ksc, BF16), vs, kwt, jnp.asarray(kwc, BF16), vw, gates)


RW_Z_W = 3 * MIX_WIDTH + 2 * LANES + 2 * LANES
RW_ROW_W = RW_Z_W + MEM_WIDTH


def _rw_prep_kernel(x_ref, g_ref, w_ref, mu_ref, w0_ref, w2_ref, a0_ref, a2_ref, g2_ref, kk_ref,
                    ka_ref, r_ref, lw_ref, kx_ref, km_ref, v_ref, a_ref, go_ref, qm_ref, carry_scr):
    W = MIX_WIDTH

    @pl.when(pl.program_id(1) == 0)
    def _():
        carry_scr[...] = jnp.zeros(carry_scr.shape, F32)

    hn = _rms(x_ref[0], g_ref[...]).astype(BF16)
    res = _mm(hn, w_ref[...])
    qm_ref[0] = res[:, RW_Z_W:].astype(BF16)
    z = res[:, :RW_Z_W]
    tm = z.shape[0]
    zprev = jnp.where(_iota((tm, 1), 0) == 0, carry_scr[0:1, :], pltpu.roll(z, 1, 0))
    carry_scr[0:1, :] = z[tm - 1:tm, :]
    z = z + (zprev - z) * mu_ref[...]
    r, k, v = z[:, :W], z[:, W:2 * W], z[:, 2 * W:3 * W]
    zw = z[:, 3 * W:3 * W + LANES]
    za = z[:, 3 * W + LANES:3 * W + 2 * LANES]
    zg = z[:, 3 * W + 2 * LANES:]
    w_log = -jax.nn.softplus(-(w0_ref[...] + _mm(jnp.tanh(zw), w2_ref[...]))) - 0.5
    a = jax.nn.sigmoid(a0_ref[...] + _mm(za, a2_ref[...]))
    r_ref[0] = r
    lw_ref[0] = -jnp.exp(w_log)
    kx_ref[0] = k * kk_ref[...]
    km_ref[0] = k * (1.0 + (a - 1.0) * ka_ref[...])
    v_ref[0] = v
    a_ref[0] = a
    go_ref[0] = _mm(jax.nn.sigmoid(zg), g2_ref[...])


def _pad_rows(w, n):
    return jnp.pad(w, ((0, n - w.shape[0]), (0, 0)))


def _rw_prep(x, norm1, w_in, mu, w0, w2, a0, a2, g2, k_k, k_a):
    B, S, D = x.shape
    W = MIX_WIDTH
    TM = min(256, S)
    o = np.cumsum([0, 3 * W, DECAY_LORA, ICLR_LORA, GATE_LORA, MEM_WIDTH])
    seg = [w_in[:, o[i]:o[i + 1]] for i in range(5)]
    padc = lambda w, n: jnp.pad(w, ((0, 0), (0, n - w.shape[1])))
    w_row = jnp.concatenate(
        [seg[0], padc(seg[1], LANES), padc(seg[2], LANES), padc(seg[3], 2 * LANES),
         seg[4] * HEAD_DIM ** -0.5], axis=1).astype(BF16)
    mus = [mu[o[i]:o[i + 1]] for i in range(4)]
    padv = lambda v, n: jnp.pad(v, (0, n - v.shape[0]))
    mu_p = jnp.concatenate([mus[0], padv(mus[1], LANES), padv(mus[2], LANES),
                            padv(mus[3], 2 * LANES)]).reshape(1, RW_Z_W)
    vec = lambda v: v.reshape(1, W)
    full = lambda a: pl.BlockSpec(a.shape, lambda b, s: (0,) * a.ndim)
    args = [norm1.reshape(1, D), w_row, mu_p, vec(w0), _pad_rows(w2, LANES).astype(BF16), vec(a0),
            _pad_rows(a2, LANES).astype(BF16), _pad_rows(g2, 2 * LANES).astype(BF16), vec(k_k), vec(k_a)]
    oblk = pl.BlockSpec((1, TM, W), lambda b, s: (b, s, 0))
    return pl.pallas_call(
        _rw_prep_kernel,
        grid=(B, S // TM),
        in_specs=[pl.BlockSpec((1, TM, D), lambda b, s: (b, s, 0))] + [full(a) for a in args],
        out_specs=[oblk] * 7 + [pl.BlockSpec((1, TM, MEM_WIDTH), lambda b, s: (b, s, 0))],
        out_shape=[jax.ShapeDtypeStruct((B, S, W), F32)] * 7
        + [jax.ShapeDtypeStruct((B, S, MEM_WIDTH), BF16)],
        scratch_shapes=[pltpu.VMEM((8, RW_Z_W), F32)],
        compiler_params=_cparams(("parallel", "arbitrary")),
        name="rw_prep",
    )(x, *args)


def _split2(x):
    hi = x.astype(BF16)
    return hi, (x - hi.astype(F32)).astype(BF16)


def _mm3(a, b):
    dot = functools.partial(jnp.dot, preferred_element_type=F32)
    return dot(a[0], b[0]) + dot(a[0], b[1]) + dot(a[1], b[0])


def _each(f, *lists):
    return [f(*args) for args in zip(*lists)]


def _tri_inverse(a2s, eye, same16, same32, same64):
    ds = _each(lambda a2: jnp.where(same16, a2, 0.0), a2s)
    xs = _each(lambda d: eye + d, ds)
    for _ in range(3):
        ds = _each(lambda d: _mm(d, d), ds)
        xs = _each(lambda x, d: x + _mm(x, d), xs, ds)
    for lo, hi_ in ((same16, same32), (same32, same64)):
        sel = hi_ & jnp.logical_not(lo)
        mids = _each(lambda x, a2: _mm(x, jnp.where(sel, a2, 0.0)), xs, a2s)
        xs = _each(lambda x, m: x + _mm(m, x), xs, mids)
    res = _each(lambda x, a2: (eye - x) + _mm3(_split2(a2), _split2(x)), xs, a2s)
    return _each(lambda x, r: x + _mm(x, r), xs, res)


def _rw_scan_kernel(r_ref, lw_ref, kx_ref, km_ref, v_ref, a_ref, g_ref, rk_ref, lnw_ref, lnb_ref,
                    o_ref, st_scr, *, TS, C):
    Dh = HEAD_DIM

    @pl.when(pl.program_id(1) == 0)
    def _():
        st_scr[...] = jnp.zeros(st_scr.shape, F32)

    head0 = _iota((1, LANES), 1) < Dh
    ltri = jnp.where(_iota((C, C), 1) <= _iota((C, C), 0), 1.0, 0.0)
    col2 = _iota((C, LANES), 1) & (Dh - 1)
    row2 = _iota((C, LANES), 0)
    m_incl = col2 <= row2
    m_strict = col2 < row2
    r128 = _iota((LANES, LANES), 0)
    c128 = _iota((LANES, LANES), 1)
    eye = jnp.where(r128 == c128, 1.0, 0.0)
    same16 = (r128 >> 4) == (c128 >> 4)
    same32 = (r128 >> 5) == (c128 >> 5)
    same64 = (r128 >> 6) == (c128 >> 6)
    zeros = jnp.zeros((C, LANES), F32)
    ltri = ltri.astype(BF16)

    def hsum(x):
        s0 = jnp.sum(jnp.where(head0, x, 0.0), axis=-1, keepdims=True)
        s1 = jnp.sum(jnp.where(head0, 0.0, x), axis=-1, keepdims=True)
        return jnp.where(head0, s0, s1)

    n_pairs = MIX_WIDTH // LANES
    N_AHEAD = 2
    lns = [slice(pi * LANES, (pi + 1) * LANES) for pi in range(n_pairs)]
    pis = list(range(n_pairs))

    def cumsum_decay(lw):
        l1 = lw.astype(BF16)
        l2 = (lw - l1.astype(F32)).astype(BF16)
        l3 = (lw - l1.astype(F32) - l2.astype(F32)).astype(BF16)
        cum3 = jnp.dot(ltri, jnp.concatenate([l1, l2, l3], axis=1), preferred_element_type=F32)
        return cum3[:, :LANES] + cum3[:, LANES:2 * LANES] + cum3[:, 2 * LANES:]

    def scaled(ln, cum, sl):
        r, lw, kx, km, a = r_ref[0, sl, ln], lw_ref[0, sl, ln], kx_ref[0, sl, ln], km_ref[0, sl, ln], a_ref[0, sl, ln]
        kk = kx / jnp.maximum(jnp.sqrt(hsum(kx * kx)), 1e-12)
        p_in = jnp.exp(cum)
        at = -kk * jnp.exp(cum - lw)
        rt = r * p_in
        p_inv = jnp.exp(-cum)
        bk = jnp.concatenate([kk * a * p_inv, km * p_inv], axis=0)
        at0, at1 = jnp.where(head0, at, 0.0), jnp.where(head0, 0.0, at)
        rt0, rt1 = jnp.where(head0, rt, 0.0), jnp.where(head0, 0.0, rt)
        lhs = jnp.concatenate([at0, at1, rt0, rt1], axis=0)
        return lhs, bk, rt, p_in[C - 1:C, :]

    def split_aa(aa):
        aa0 = jnp.where(m_strict, aa[0:C], 0.0)
        aa1 = pltpu.roll(jnp.where(m_strict, aa[C:2 * C], 0.0), Dh, 1)
        ar0 = jnp.where(m_incl, aa[2 * C:3 * C], 0.0)
        ar1 = jnp.where(m_incl, aa[3 * C:4 * C], 0.0)
        a2 = jnp.concatenate([jnp.where(head0, aa0, 0.0), jnp.where(head0, 0.0, aa1)], axis=0)
        return aa0, aa1, ar0, ar1, a2

    def epilogue(ln, y, sl):
        r, km, v = r_ref[0, sl, ln], km_ref[0, sl, ln], v_ref[0, sl, ln]
        mean = hsum(y) * (1.0 / Dh)
        d = y - mean
        var = hsum(d * d) * (1.0 / Dh)
        yn = d * lax.rsqrt(var + GN_EPS) * lnw_ref[:, ln] + lnb_ref[:, ln]
        bonus = hsum(r * km * rk_ref[:, ln]) * v
        o_ref[0, sl, ln] = ((yn + bonus) * g_ref[0, sl, ln]).astype(BF16)

    def chunk_group(cg, carry):
        sls = [pl.ds(pl.multiple_of((cg * N_AHEAD + j) * C, C), C) for j in range(N_AHEAD)]
        sl_i = [sl for sl in sls for _ in lns]
        ln_i = [ln for _ in sls for ln in lns]
        cums = _each(lambda sl, ln: cumsum_decay(lw_ref[0, sl, ln]), sl_i, ln_i)
        lhss, bks, rts, pcs = zip(*_each(scaled, ln_i, cums, sl_i))
        aas = _each(lambda lhs, bk: _mm(lhs, bk, _NT), lhss, bks)
        aa0s, aa1s, ar0s, ar1s, a2s = zip(*_each(split_aa, aas))
        t2s = _tri_inverse(a2s, eye, same16, same32, same64)
        vs = _each(lambda sl, ln: v_ref[0, sl, ln], sl_i, ln_i)
        x0s = _each(lambda aa0, v: _mm(aa0, jnp.concatenate([zeros, v], axis=0)), aa0s, vs)
        x1s = _each(lambda aa1, v: _mm(aa1, jnp.concatenate([v, zeros], axis=0)), aa1s, vs)
        wus = _each(lambda t2, lhs, x0, x1: _mm(t2, jnp.concatenate(
            [lhs[:2 * C], jnp.concatenate([x0, x1], axis=0)], axis=1)), t2s, lhss, x0s, x1s)
        for j, sl in enumerate(sls):
            k = slice(j * n_pairs, (j + 1) * n_pairs)
            sts = _each(lambda pi: st_scr[pi], pis)
            us = _each(lambda wu, st: _mm(wu[:C, :LANES] + wu[C:, :LANES], st, _NT)
                       + jnp.where(head0, wu[:C, LANES:], wu[C:, LANES:]), wus[k], sts)
            uvs = _each(lambda u, v: jnp.concatenate([u, v], axis=0), us, vs[k])
            ys = _each(lambda rt, st, ar0, ar1, uv: _mm(rt, st, _NT)
                       + jnp.where(head0, _mm(ar0, uv), _mm(ar1, uv)), rts[k], sts, ar0s[k], ar1s[k], uvs)
            new = _each(lambda st, pc, uv, bk: st * pc + jnp.where(same64, _mm(uv, bk * pc, _TN), 0.0),
                        sts, pcs[k], uvs, bks[k])
            for pi in pis:
                st_scr[pi] = new[pi]
            _each(lambda ln, y: epilogue(ln, y, sl), lns, ys)
        return carry

    lax.fori_loop(0, TS // (C * N_AHEAD), chunk_group, 0)


def _rw_scan(r, lw, kx, km, v, a, g, r_k, lnx_w, lnx_b):
    B, S, W = r.shape
    TS = min(256, S)
    C = 64
    blk = pl.BlockSpec((1, TS, W), lambda b, s: (b, s, 0))
    vblk = pl.BlockSpec((1, W), lambda b, s: (0, 0))
    kern = functools.partial(_rw_scan_kernel, TS=TS, C=C)
    return pl.pallas_call(
        kern,
        grid=(B, S // TS),
        in_specs=[blk] * 7 + [vblk] * 3,
        out_specs=blk,
        out_shape=jax.ShapeDtypeStruct((B, S, W), BF16),
        scratch_shapes=[pltpu.VMEM((W // LANES, LANES, LANES), F32)],
        compiler_params=_cparams(("parallel", "arbitrary")),
        name="rw_scan",
    )(r, lw, kx, km, v, a, g, r_k.reshape(1, W), lnx_w.reshape(1, W), lnx_b.reshape(1, W))


def kernel(x, mem, norm1, norm_mem, w_mem_kv, w_o, norm2, w_ffn_in, w_ffn_out, nsa_w_in, nsa_gate_b,
           nsa_cmp_pos, nsa_cmp_w1, nsa_cmp_w2, rw_w_in, rw_mu, rw_w0, rw_w2, rw_a0, rw_a2, rw_g2,
           rw_k_k, rw_k_a, rw_r_k, rw_lnx_w, rw_lnx_b, final_norm):
    depth = norm1.shape[0]
    B, S, _ = x.shape
    ktm, vm = _mem_kv(mem, norm_mem, w_mem_kv)
    for i in range(depth):
        j = i // 2
        if i % 2 == 0:
            q, kc, vc, vs, vw, kst, kwt, qm, gates = _nsa_proj(x, norm1[i], nsa_w_in[j], nsa_gate_b[j])
            kcb, vcb = _compress(kc, vc, nsa_cmp_pos[j], nsa_cmp_w1[j], nsa_cmp_w2[j])
            mix = _nsa_attn(q, kcb, vcb, kst, vs, kwt, vw, gates)
        else:
            r, lw, kx, km, v, a, g, qm = _rw_prep(x, norm1[i], rw_w_in[j], rw_mu[j], rw_w0[j], rw_w2[j],
                                                  rw_a0[j], rw_a2[j], rw_g2[j], rw_k_k[j], rw_k_a[j])
            mix = _rw_scan(r, lw, kx, km, v, a, g, rw_r_k[j], rw_lnx_w[j], rw_lnx_b[j])
            mix = mix.reshape(B, 1, S, MIX_WIDTH)
        cross = _mem_attn(qm, ktm[i], vm[i])
        x = _post(x, mix, cross, w_o[i], norm2[i], w_ffn_in[i], w_ffn_out[i], final_norm,
                  final=(i == depth - 1))
    return x
```

```python
import functools
import math

import numpy as np
import jax
import jax.numpy as jnp
from jax import lax
from jax.experimental import pallas as pl
from jax.experimental.pallas import tpu as pltpu

F32 = jnp.float32
BF16 = jnp.bfloat16
HI = lax.Precision.HIGHEST

D_MODEL = 1024
HEAD_DIM = 64
MIX_WIDTH = 768
MIX_HEADS = 12
MEM_HEADS = 4
MEM_WIDTH = 256
KV_GROUPS = 4
GROUP_SIZE = 3
KV_WIDTH = 256
CMP_BLOCK = 32
CMP_STRIDE = 16
CMP_HIDDEN = 128
SEL_BLOCK = 64
SEL_TOPN = 16
WINDOW = 512
FORCE_SCORE = 1.0e4
DECAY_LORA = 64
ICLR_LORA = 64
GATE_LORA = 160
GN_EPS = 64e-5
FFN_HIDDEN = 2816
RMS_EPS = 1e-6
NEG_INF = -1e30
LOG2E = 1.4426950408889634

LANES = 128
VMEM_LIMIT = 56 * 1024 * 1024

_NT = (((1,), (1,)), ((), ()))
_TN = (((0,), (0,)), ((), ()))


def _mm(a, b, dims=None, hi=False):
    if hi:
        a, b, prec = a.astype(F32), b.astype(F32), HI
    else:
        a, b, prec = a.astype(BF16), b.astype(BF16), None
    if dims is None:
        return jnp.dot(a, b, preferred_element_type=F32, precision=prec)
    return lax.dot_general(a, b, dims, preferred_element_type=F32, precision=prec)


def _iota(shape, dim):
    return lax.broadcasted_iota(jnp.int32, shape, dim)


def _rms(x, g):
    ms = jnp.mean(x * x, axis=-1, keepdims=True)
    return x * lax.rsqrt(ms + RMS_EPS) * g


def _cparams(sem):
    return pltpu.CompilerParams(dimension_semantics=sem, vmem_limit_bytes=VMEM_LIMIT)


def _mem_kv_kernel(mem_ref, g_ref, wkt_ref, wv_ref, kt_ref, v_ref):
    mn = _rms(mem_ref[0], g_ref[0]).astype(BF16)
    kt = _mm(wkt_ref[0], mn, _NT)
    v = _mm(mn, wv_ref[0])
    rowh = _iota(kt.shape, 0) // HEAD_DIM
    colh = _iota(v.shape, 1) // HEAD_DIM
    for h in range(MEM_HEADS):
        kt_ref[0, 0, h] = jnp.where(rowh == h, kt, 0.0).astype(BF16)
        v_ref[0, 0, h] = jnp.where(colh == h, v, 0.0).astype(BF16)


def _mem_kv(mem, norm_mem, w_mem_kv):
    B, M, D = mem.shape
    L = norm_mem.shape[0]
    wkt = jnp.swapaxes(w_mem_kv[:, :, :MEM_WIDTH], 1, 2).astype(BF16)
    wv = w_mem_kv[:, :, MEM_WIDTH:].astype(BF16)
    return pl.pallas_call(
        _mem_kv_kernel,
        grid=(L, B),
        in_specs=[
            pl.BlockSpec((1, M, D), lambda l, b: (b, 0, 0)),
            pl.BlockSpec((1, 1, D), lambda l, b: (l, 0, 0)),
            pl.BlockSpec((1, MEM_WIDTH, D), lambda l, b: (l, 0, 0)),
            pl.BlockSpec((1, D, MEM_WIDTH), lambda l, b: (l, 0, 0)),
        ],
        out_specs=[
            pl.BlockSpec((1, 1, MEM_HEADS, MEM_WIDTH, M), lambda l, b: (l, b, 0, 0, 0)),
            pl.BlockSpec((1, 1, MEM_HEADS, M, MEM_WIDTH), lambda l, b: (l, b, 0, 0, 0)),
        ],
        out_shape=[
            jax.ShapeDtypeStruct((L, B, MEM_HEADS, MEM_WIDTH, M), BF16),
            jax.ShapeDtypeStruct((L, B, MEM_HEADS, M, MEM_WIDTH), BF16),
        ],
        compiler_params=_cparams(("parallel", "parallel")),
        name="mem_kv",
    )(mem, norm_mem.reshape(L, 1, D), wkt, wv)


def _mem_attn_kernel(q_ref, kt_ref, v_ref, o_ref):
    q = q_ref[0]
    acc = jnp.zeros(q.shape, F32)
    for h in range(MEM_HEADS):
        s = _mm(q, kt_ref[0, h])
        m = jnp.max(s, axis=-1, keepdims=True)
        p = jnp.exp(s - m)
        l = jnp.sum(p, axis=-1, keepdims=True)
        acc = acc + _mm(p * (1.0 / l), v_ref[0, h])
    o_ref[0] = acc.astype(BF16)


def _mem_attn(qm, ktm, vm):
    B, S, _ = qm.shape
    M = ktm.shape[-1]
    TM = min(512, S)
    return pl.pallas_call(
        _mem_attn_kernel,
        grid=(B, S // TM),
        in_specs=[
            pl.BlockSpec((1, TM, MEM_WIDTH), lambda b, s: (b, s, 0)),
            pl.BlockSpec((1, MEM_HEADS, MEM_WIDTH, M), lambda b, s: (b, 0, 0, 0)),
            pl.BlockSpec((1, MEM_HEADS, M, MEM_WIDTH), lambda b, s: (b, 0, 0, 0)),
        ],
        out_specs=pl.BlockSpec((1, TM, MEM_WIDTH), lambda b, s: (b, s, 0)),
        out_shape=jax.ShapeDtypeStruct((B, S, MEM_WIDTH), BF16),
        compiler_params=_cparams(("parallel", "parallel")),
        name="mem_attn",
    )(qm, ktm, vm)


def _post_kernel(x_ref, mix_ref, cross_ref, wom_ref, woc_ref, g2_ref, wg_ref, wu_ref, wout_ref,
                 gf_ref, o_ref, x1_scr, hn_scr, acc_scr, *, n_parts, n_h, final):
    h = pl.program_id(2)

    @pl.when(h == 0)
    def _():
        x1 = x_ref[0] + _mm(cross_ref[0], woc_ref[...])
        for p in range(n_parts):
            x1 = x1 + _mm(mix_ref[0, p], wom_ref[p])
        x1_scr[...] = x1
        hn_scr[...] = _rms(x1, g2_ref[...]).astype(BF16)
        acc_scr[...] = jnp.zeros(acc_scr.shape, F32)

    hn = hn_scr[...]
    gate = _mm(hn, wg_ref[...])
    up = _mm(hn, wu_ref[...])
    hid = gate * jax.nn.sigmoid(gate) * up
    acc_scr[...] += _mm(hid, wout_ref[...])

    @pl.when(h == n_h - 1)
    def _():
        y = x1_scr[...] + acc_scr[...]
        if final:
            y = _rms(y, gf_ref[...])
        o_ref[0] = y


def _post(x, mix, cross, w_o, norm2, w_ffn_in, w_ffn_out, final_norm, final):
    B, S, D = x.shape
    P, W = mix.shape[1], mix.shape[3]
    TM = min(1024, S)
    TH = 256
    NH = FFN_HIDDEN // TH
    wom = w_o[:MIX_WIDTH].reshape(P, W, D).astype(BF16)
    woc = w_o[MIX_WIDTH:].astype(BF16)
    wi = w_ffn_in.astype(BF16)
    wo = w_ffn_out.astype(BF16)
    kern = functools.partial(_post_kernel, n_parts=P, n_h=NH, final=final)
    return pl.pallas_call(
        kern,
        grid=(B, S // TM, NH),
        in_specs=[
            pl.BlockSpec((1, TM, D), lambda b, s, h: (b, s, 0)),
            pl.BlockSpec((1, P, TM, W), lambda b, s, h: (b, 0, s, 0)),
            pl.BlockSpec((1, TM, MEM_WIDTH), lambda b, s, h: (b, s, 0)),
            pl.BlockSpec((P, W, D), lambda b, s, h: (0, 0, 0)),
            pl.BlockSpec((MEM_WIDTH, D), lambda b, s, h: (0, 0)),
            pl.BlockSpec((1, D), lambda b, s, h: (0, 0)),
            pl.BlockSpec((D, TH), lambda b, s, h: (0, h)),
            pl.BlockSpec((D, TH), lambda b, s, h: (0, NH + h)),
            pl.BlockSpec((TH, D), lambda b, s, h: (h, 0)),
            pl.BlockSpec((1, D), lambda b, s, h: (0, 0)),
        ],
        out_specs=pl.BlockSpec((1, TM, D), lambda b, s, h: (b, s, 0)),
        out_shape=jax.ShapeDtypeStruct((B, S, D), F32),
        scratch_shapes=[
            pltpu.VMEM((TM, D), F32),
            pltpu.VMEM((TM, D), BF16),
            pltpu.VMEM((TM, D), F32),
        ],
        compiler_params=_cparams(("parallel", "parallel", "arbitrary")),
        name="post_ffn",
    )(x, mix, cross, wom, woc, norm2.reshape(1, D), wi, wi, wo, final_norm.reshape(1, D))


NSA_ROW_W = MIX_WIDTH + 4 * KV_WIDTH + MEM_WIDTH + LANES


def _nsa_proj_kernel(x_ref, g_ref, w_ref, wt_ref, gb_ref, qc_ref, q_ref, kc_ref, vc_ref, vs_ref, vw_ref,
                     kst_ref, kwt_ref, qm_ref, gate_ref):
    hn = _rms(x_ref[0], g_ref[...]).astype(BF16)
    res = _mm(hn, w_ref[...])
    tm = res.shape[0]
    for h in range(MIX_HEADS):
        q_ref[0, h] = jnp.concatenate(
            [res[:, h * HEAD_DIM:(h + 1) * HEAD_DIM], jnp.broadcast_to(qc_ref[h], (tm, HEAD_DIM))],
            axis=1).astype(BF16)
    ones_col = jnp.ones((tm, HEAD_DIM), F32)
    off = MIX_WIDTH
    for ref, with_ones in ((kc_ref, False), (vc_ref, False), (vs_ref, True), (vw_ref, True)):
        for g in range(KV_GROUPS):
            t = res[:, off + g * HEAD_DIM: off + (g + 1) * HEAD_DIM]
            if with_ones:
                t = jnp.concatenate([t, ones_col], axis=1)
            ref[0, g] = t.astype(BF16)
        off += KV_WIDTH
    qm_ref[0] = res[:, off:off + MEM_WIDTH].astype(BF16)
    off += MEM_WIDTH
    gates = jax.nn.sigmoid(res[:, off:off + 64] + gb_ref[...])
    for g in range(KV_GROUPS):
        gate_ref[0, g] = gates[:, g * 16:(g + 1) * 16]
    rt = _mm(wt_ref[...], hn, _NT)
    kst_ref[0] = rt[:KV_WIDTH].astype(BF16)
    kwt_ref[0] = rt[KV_WIDTH:].astype(BF16)


def _alibi_slopes(n):
    def pow2(m):
        start = 2.0 ** (-8.0 / m)
        return [start ** (i + 1) for i in range(m)]
    c = 2 ** int(math.floor(math.log2(n)))
    s = pow2(c)
    if c < n:
        s = s + pow2(2 * c)[0::2][: n - c]
    return np.asarray(s, dtype=np.float32)


def _pos_pieces(pos):
    a64 = (pos >> 6) * 64
    b = pos & 63
    return np.stack([a64, a64, a64, b, b, b]).astype(np.float32)


def _slope_pieces():
    sl = jnp.asarray(_alibi_slopes(MIX_HEADS) * np.float32(LOG2E), F32)
    s1 = sl.astype(BF16).astype(F32)
    s2 = (sl - s1).astype(BF16).astype(F32)
    s3 = (sl - s1 - s2).astype(BF16).astype(F32)
    six = jnp.stack([s1, s2, s3, s1, s2, s3], axis=1)
    return jnp.pad(six, ((0, 0), (0, HEAD_DIM - 6))).reshape(MIX_HEADS, 1, HEAD_DIM)


def _nsa_proj(x, norm1, w_in, gate_b):
    B, S, D = x.shape
    TM = min(512, S)
    G, H, Dh = KV_GROUPS, MIX_HEADS, HEAD_DIM
    scale = HEAD_DIM ** -0.5
    o = np.cumsum([0, MIX_WIDTH] + [KV_WIDTH] * 6 + [3 * MIX_HEADS, MEM_WIDTH])
    wq, wkc, wvc, wks, wvs, wkw, wvw, wgl, wqm = (w_in[:, o[i]:o[i + 1]] for i in range(9))
    wgl = wgl.reshape(D, 3, G, GROUP_SIZE).transpose(0, 2, 1, 3).reshape(D, G, 9)
    wgl = jnp.pad(wgl, ((0, 0), (0, 0), (0, 7))).reshape(D, 64)
    gb = gate_b.reshape(3, G, GROUP_SIZE).transpose(1, 0, 2).reshape(G, 9)
    gb = jnp.pad(gb, ((0, 0), (0, 7))).reshape(1, 64)
    w_row = jnp.concatenate(
        [wq * (scale * LOG2E), wkc, wvc, wvs, wvw, wqm * scale, wgl, jnp.zeros((D, LANES - 64), F32)],
        axis=1).astype(BF16)
    w_t = jnp.concatenate([wks, wkw], axis=1).T.astype(BF16)
    head = lambda n, w: pl.BlockSpec((1, n, TM, w), lambda b, s: (b, 0, s, 0))
    return pl.pallas_call(
        _nsa_proj_kernel,
        grid=(B, S // TM),
        in_specs=[
            pl.BlockSpec((1, TM, D), lambda b, s: (b, s, 0)),
            pl.BlockSpec((1, D), lambda b, s: (0, 0)),
            pl.BlockSpec((D, NSA_ROW_W), lambda b, s: (0, 0)),
            pl.BlockSpec((2 * KV_WIDTH, D), lambda b, s: (0, 0)),
            pl.BlockSpec((1, 64), lambda b, s: (0, 0)),
            pl.BlockSpec((H, 1, Dh), lambda b, s: (0, 0, 0)),
        ],
        out_specs=[
            head(H, LANES), head(G, Dh), head(G, Dh), head(G, LANES), head(G, LANES),
            pl.BlockSpec((1, KV_WIDTH, TM), lambda b, s: (b, 0, s)),
            pl.BlockSpec((1, KV_WIDTH, TM), lambda b, s: (b, 0, s)),
            pl.BlockSpec((1, TM, MEM_WIDTH), lambda b, s: (b, s, 0)),
            pl.BlockSpec((1, G, TM, 16), lambda b, s: (b, 0, s, 0)),
        ],
        out_shape=[
            jax.ShapeDtypeStruct((B, H, S, LANES), BF16),
            jax.ShapeDtypeStruct((B, G, S, Dh), BF16),
            jax.ShapeDtypeStruct((B, G, S, Dh), BF16),
            jax.ShapeDtypeStruct((B, G, S, LANES), BF16),
            jax.ShapeDtypeStruct((B, G, S, LANES), BF16),
            jax.ShapeDtypeStruct((B, KV_WIDTH, S), BF16),
            jax.ShapeDtypeStruct((B, KV_WIDTH, S), BF16),
            jax.ShapeDtypeStruct((B, S, MEM_WIDTH), BF16),
            jax.ShapeDtypeStruct((B, G, S, 16), F32),
        ],
        compiler_params=_cparams(("parallel", "parallel")),
        name="nsa_proj",
    )(x, norm1.reshape(1, D), w_row, w_t, gb, _slope_pieces())


def _compress_kernel(kc_ref, vc_ref, w1_ref, w2_ref, pos_ref, cpos_ref, kcb_ref, vcb_ref):
    half = CMP_STRIDE * HEAD_DIM
    for idx, (src, dst) in enumerate(((kc_ref, kcb_ref), (vc_ref, vcb_ref))):
        c = src[0, 0]
        ncp = c.shape[0]
        a = _mm(c, w1_ref[idx, :half])
        bm = _mm(c, w1_ref[idx, half:])
        bias = _mm(pos_ref[idx], w1_ref[idx])[0:1]
        hid = jax.nn.gelu(a + pltpu.roll(bm, ncp - 1, 0) + bias)
        out = _mm(hid, w2_ref[idx])
        out = jnp.where(_iota(out.shape, 0) < ncp - 1, out, 0.0)
        pad = cpos_ref[...] if idx == 0 else jnp.zeros(out.shape, F32)
        out = jnp.concatenate([out, pad], axis=1)
        dst[0, 0] = out.astype(BF16)


def _compress(kc, vc, cmp_pos, cmp_w1, cmp_w2):
    B, G, S, Dh = kc.shape
    NCP = S // CMP_STRIDE
    kcr = kc.reshape(B, G, NCP, CMP_STRIDE * Dh)
    vcr = vc.reshape(B, G, NCP, CMP_STRIDE * Dh)
    w1 = cmp_w1.reshape(2, CMP_BLOCK * Dh, CMP_HIDDEN).astype(BF16)
    w2 = cmp_w2.astype(BF16)
    pos = jnp.broadcast_to(cmp_pos.reshape(2, 1, CMP_BLOCK * Dh), (2, 8, CMP_BLOCK * Dh)).astype(BF16)
    cend = np.arange(NCP) * CMP_STRIDE + CMP_BLOCK - 1
    cpos = np.zeros((NCP, Dh), np.float32)
    cpos[:, :6] = _pos_pieces(cend).T
    blk = pl.BlockSpec((1, 1, NCP, CMP_STRIDE * Dh), lambda b, g: (b, g, 0, 0))
    return pl.pallas_call(
        _compress_kernel,
        grid=(B, G),
        in_specs=[
            blk, blk,
            pl.BlockSpec((2, CMP_BLOCK * Dh, CMP_HIDDEN), lambda b, g: (0, 0, 0)),
            pl.BlockSpec((2, CMP_HIDDEN, Dh), lambda b, g: (0, 0, 0)),
            pl.BlockSpec((2, 8, CMP_BLOCK * Dh), lambda b, g: (0, 0, 0)),
            pl.BlockSpec((NCP, Dh), lambda b, g: (0, 0)),
        ],
        out_specs=[pl.BlockSpec((1, 1, NCP, LANES), lambda b, g: (b, g, 0, 0))] * 2,
        out_shape=[jax.ShapeDtypeStruct((B, G, NCP, LANES), BF16)] * 2,
        compiler_params=_cparams(("parallel", "parallel")),
        name="nsa_compress",
    )(kcr, vcr, w1, w2, pos, jnp.asarray(cpos))


def _exp2_bf16(s, m):
    return jnp.concatenate(
        [jnp.exp2((s[:, i * LANES:(i + 1) * LANES] - m).astype(BF16)) for i in range(s.shape[1] // LANES)],
        axis=1)


def _nsa_attn_kernel(q_ref, kcb_ref, vcb_ref, ovt_ref, kst_ref, ksc_ref, vs_ref, kwt_ref, kwc_ref,
                     vw_ref, gate_ref, eg_ref, o_ref, score_scr, m_scr, acc_scr, s_scr, *, TQ, TK, S, n_sel, NG):
    R, Dh, CH = GROUP_SIZE, HEAD_DIM, LANES
    t0 = pl.program_id(2) * TQ
    NCP, NB = S // CMP_STRIDE, S // SEL_BLOCK
    n_sub = TQ // CH
    n_ch = R * n_sub
    groups = list(range(NG))
    chunks = [(g, c) for g in groups for c in range(n_ch)]
    rows_of = lambda c: slice(c * CH, (c + 1) * CH)
    qa = [q_ref[0, g * R:(g + 1) * R].reshape(R * TQ, LANES) for g in groups]
    tq_col = t0 + _iota((TQ, 1), 0)
    tq_col3 = t0 + (_iota((R * TQ, 1), 0) & (TQ - 1))

    sc = [_mm(qa[g], kcb_ref[0, g], _NT) for g in groups]
    n_row = _iota((1, NCP), 1)
    cend = n_row * CMP_STRIDE + (CMP_BLOCK - 1)
    mask_c = (cend <= tq_col) & (n_row < NCP - 1)
    psum = [[jnp.zeros((CH, NCP), F32) for _ in range(n_sub)] for _ in groups]
    o_c = []
    for g, c in chunks:
        mk = mask_c[(c % n_sub) * CH:(c % n_sub + 1) * CH]
        s = jnp.where(mk, sc[g][rows_of(c)], NEG_INF)
        m = jnp.max(s, axis=-1, keepdims=True)
        p = jnp.where(mk, jnp.exp2(s - m), 0.0)
        l = jnp.sum(p, axis=-1, keepdims=True)
        p = p * jnp.where(l > 0.0, 1.0 / l, 0.0)
        psum[g][c % n_sub] = psum[g][c % n_sub] + p
        o_c.append(_mm(p, vcb_ref[0, g]))
    psum = [jnp.concatenate(psum[g], axis=0) for g in groups]
    o_c = [jnp.concatenate(o_c[g * n_ch:(g + 1) * n_ch], axis=0) for g in groups]

    tb = lax.shift_right_logical(t0 + _iota((1, TQ), 1), 6)
    j_col = _iota((NB, 1), 0)
    valid = j_col <= tb
    forced = (j_col == 0) | (j_col == tb) | (j_col == tb - 1)
    score = []
    for g in groups:
        imp_t = _mm(ovt_ref[...], psum[g], _NT, hi=True)
        score.append(jnp.where(valid, jnp.where(forced, FORCE_SCORE, imp_t), -jnp.inf))
        score_scr[g] = score[g]
    n_valid = (t0 + TQ - 1) // SEL_BLOCK + 1

    def rank_body(i, cnts):
        tie = jnp.where(i < j_col, 1.0, 0.0)
        out = []
        for g in groups:
            row = score_scr[g, pl.ds(i, 1), :]
            out.append(cnts[g] + jnp.where(row > score[g], 1.0, jnp.where(row == score[g], tie, 0.0)))
        return tuple(out)

    rank = lax.fori_loop(0, n_valid, rank_body, tuple(jnp.zeros((NB, TQ), F32) for _ in groups))
    q2 = []
    for g in groups:
        unsel_t = jnp.where(valid, jnp.where(rank[g] < n_sel, 0.0, 1.0), 1.0)
        unsel = jnp.concatenate([unsel_t, jnp.zeros((LANES - NB, TQ), F32)], axis=0).T.astype(BF16)
        q2.append(jnp.concatenate([qa[g], jnp.concatenate([unsel] * R, axis=0)], axis=1))

    def scores_to(slot, kt):
        k0 = pl.multiple_of(kt * TK, TK)
        consts = ksc_ref[:, pl.ds(k0, TK)]
        for g in groups:
            s_scr[slot, g] = jnp.dot(q2[g], jnp.concatenate(
                [kst_ref[0, g * Dh:(g + 1) * Dh, pl.ds(k0, TK)], consts], axis=0), preferred_element_type=F32)

    scores_to(0, 0)

    WK = WINDOW + TQ
    w0 = pl.multiple_of(jnp.maximum(t0 - WINDOW, 0), LANES)
    kpos_w = w0 + _iota((1, WK), 1)
    mask_w = []
    for h in range(n_sub):
        dist = (t0 + h * CH + _iota((CH, 1), 0)) - kpos_w
        mask_w.append((dist >= 0) & (dist < WINDOW))
    consts_w = kwc_ref[:, pl.ds(w0, WK)]
    sw = [jnp.dot(qa[g], jnp.concatenate(
        [kwt_ref[0, g * Dh:(g + 1) * Dh, pl.ds(w0, WK)], consts_w], axis=0),
        preferred_element_type=F32) for g in groups]
    ow = []
    for g, c in chunks:
        s = jnp.where(mask_w[c % n_sub], sw[g][rows_of(c)], NEG_INF)
        m = jnp.broadcast_to(jnp.max(s, axis=-1, keepdims=True), (CH, LANES))
        ow.append(jnp.dot(_exp2_bf16(s, m), vw_ref[0, g, pl.ds(w0, WK), :], preferred_element_type=F32))
    ow = [jnp.concatenate(ow[g * n_ch:(g + 1) * n_ch], axis=0) for g in groups]

    m_scr[...] = jnp.full(m_scr.shape, NEG_INF, F32)
    acc_scr[...] = jnp.zeros(acc_scr.shape, F32)
    n_kt = (t0 + TQ - 1) // TK + 1

    def accumulate(kt, slot, causal):
        k0 = pl.multiple_of(kt * TK, TK)

        def tile(g, c):
            x = s_scr[slot, g, rows_of(c), :]
            if causal:
                keep = k0 + _iota((1, TK), 1) <= t0 + (c % n_sub) * CH + _iota((CH, 1), 0)
                x = jnp.where(keep, x, NEG_INF)
            return x

        m_old = [m_scr[g, rows_of(c)] for g, c in chunks]
        m_new = [jnp.maximum(mo, jnp.max(tile(g, c), axis=-1, keepdims=True))
                 for mo, (g, c) in zip(m_old, chunks)]
        for i, (g, c) in enumerate(chunks):
            p = _exp2_bf16(tile(g, c), m_new[i])
            pv = jnp.dot(p, vs_ref[0, g, pl.ds(k0, TK), :], preferred_element_type=F32)
            acc_scr[g, rows_of(c)] = jnp.exp2(m_old[i] - m_new[i]) * acc_scr[g, rows_of(c)] + pv
            m_scr[g, rows_of(c)] = m_new[i]

    n_full = n_kt - 1

    def pair_body(j, carry):
        scores_to(1, 2 * j + 1)
        accumulate(2 * j, 0, False)
        scores_to(0, 2 * j + 2)
        accumulate(2 * j + 1, 1, False)
        return carry

    lax.fori_loop(0, n_full // 2, pair_body, 0)

    @pl.when(n_full % 2 == 1)
    def _():
        scores_to(1, n_full)
        accumulate(n_full - 1, 0, False)
        accumulate(n_full, 1, True)

    @pl.when(n_full % 2 == 0)
    def _():
        accumulate(n_full, 0, True)

    for g in groups:
        acc = acc_scr[g]
        o_s = acc * (1.0 / pltpu.roll(acc, Dh, 1))
        o_w = ow[g] * (1.0 / pltpu.roll(ow[g], Dh, 1))
        gv = gate_ref[0, g]
        g_hi = gv.astype(BF16)
        g_lo = (gv - g_hi.astype(F32)).astype(BF16)
        ge = (jnp.dot(g_hi, eg_ref[...], preferred_element_type=F32)
              + jnp.dot(g_lo, eg_ref[...], preferred_element_type=F32))
        gate = lambda k: ge[:, k * LANES:(k + 1) * LANES]
        for r in range(R):
            rows = slice(r * TQ, (r + 1) * TQ)
            out = gate(r) * o_c[g][rows] + gate(R + r) * o_s[rows] + gate(2 * R + r) * o_w[rows]
            o_ref[0, g, :, r * Dh:(r + 1) * Dh] = out[:, :Dh].astype(BF16)


def _nsa_attn(q, kcb, vcb, kst, vs, kwt, vw, gates):
    B, H, S, _ = q.shape
    G, R, Dh = KV_GROUPS, GROUP_SIZE, HEAD_DIM
    TQ = 256
    TK = min(512, S)
    NG = 2
    NCP, NB = S // CMP_STRIDE, S // SEL_BLOCK
    n_sel = min(SEL_TOPN, NB)
    assert S % TK == 0 and S >= WINDOW + TQ and NB <= Dh and S <= 4096
    cs = np.arange(NCP) * CMP_STRIDE
    ss = np.arange(NB) * SEL_BLOCK
    ov = (cs[:, None] <= ss[None, :] + SEL_BLOCK - 1) & (cs[:, None] + CMP_BLOCK - 1 >= ss[None, :])
    ov[NCP - 1] = False
    ovt = jnp.asarray(ov.T.astype(np.float32))
    pieces = _pos_pieces(np.arange(S))
    ksc = np.zeros((3 * Dh, S), np.float32)
    ksc[:6] = pieces
    ksc[Dh:Dh + NB] = np.where(np.arange(S)[None, :] // SEL_BLOCK == np.arange(NB)[:, None], NEG_INF, 0.0)
    kwc = np.zeros((Dh, S), np.float32)
    kwc[:6] = pieces
    eg = (np.arange(9 * LANES)[None, :] // LANES == np.arange(16)[:, None]).astype(np.float32)
    kern = functools.partial(_nsa_attn_kernel, TQ=TQ, TK=TK, S=S, n_sel=n_sel, NG=NG)
    return pl.pallas_call(
        kern,
        grid=(B, G // NG, S // TQ),
        in_specs=[
            pl.BlockSpec((1, NG * R, TQ, LANES), lambda b, g, i: (b, g, i, 0)),
            pl.BlockSpec((1, NG, NCP, LANES), lambda b, g, i: (b, g, 0, 0)),
            pl.BlockSpec((1, NG, NCP, LANES), lambda b, g, i: (b, g, 0, 0)),
            pl.BlockSpec((NB, NCP), lambda b, g, i: (0, 0)),
            pl.BlockSpec((1, NG * Dh, S), lambda b, g, i: (b, g, 0)),
            pl.BlockSpec((3 * Dh, S), lambda b, g, i: (0, 0)),
            pl.BlockSpec((1, NG, S, LANES), lambda b, g, i: (b, g, 0, 0)),
            pl.BlockSpec((1, NG * Dh, S), lambda b, g, i: (b, g, 0)),
            pl.BlockSpec((Dh, S), lambda b, g, i: (0, 0)),
            pl.BlockSpec((1, NG, S, LANES), lambda b, g, i: (b, g, 0, 0)),
            pl.BlockSpec((1, NG, TQ, 16), lambda b, g, i: (b, g, i, 0)),
            pl.BlockSpec((16, 9 * LANES), lambda b, g, i: (0, 0)),
        ],
        out_specs=pl.BlockSpec((1, NG, TQ, R * Dh), lambda b, g, i: (b, g, i, 0)),
        out_shape=jax.ShapeDtypeStruct((B, G, S, R * Dh), BF16),
        scratch_shapes=[
            pltpu.VMEM((NG, NB, TQ), F32),
            pltpu.VMEM((NG, R * TQ, LANES), F32),
            pltpu.VMEM((NG, R * TQ, LANES), F32),
            pltpu.VMEM((2, NG, R * TQ, TK), F32),
        ],
        compiler_params=_cparams(("parallel", "parallel", "arbitrary")),
        name="nsa_attn",
    )(q, kcb, vcb, ovt, kst, jnp.asarray(ksc, BF16), vs, kwt, jnp.asarray(kwc, BF16), vw, gates,
      jnp.asarray(eg, BF16))


RW_Z_W = 3 * MIX_WIDTH + 2 * LANES + 2 * LANES
RW_ROW_W = RW_Z_W + MEM_WIDTH


def _rw_prep_kernel(x_ref, g_ref, w_ref, mu_ref, w0_ref, w2_ref, a0_ref, a2_ref, g2_ref, kk_ref,
                    ka_ref, r_ref, lw_ref, kx_ref, km_ref, v_ref, a_ref, go_ref, qm_ref, carry_scr):
    W = MIX_WIDTH

    @pl.when(pl.program_id(1) == 0)
    def _():
        carry_scr[...] = jnp.zeros(carry_scr.shape, F32)

    hn = _rms(x_ref[0], g_ref[...]).astype(BF16)
    res = _mm(hn, w_ref[...])
    qm_ref[0] = res[:, RW_Z_W:].astype(BF16)
    z = res[:, :RW_Z_W]
    tm = z.shape[0]
    zprev = jnp.where(_iota((tm, 1), 0) == 0, carry_scr[0:1, :], pltpu.roll(z, 1, 0))
    carry_scr[0:1, :] = z[tm - 1:tm, :]
    z = z + (zprev - z) * mu_ref[...]
    r, k, v = z[:, :W], z[:, W:2 * W], z[:, 2 * W:3 * W]
    zw = z[:, 3 * W:3 * W + LANES]
    za = z[:, 3 * W + LANES:3 * W + 2 * LANES]
    zg = z[:, 3 * W + 2 * LANES:]
    w_log = -jax.nn.softplus(-(w0_ref[...] + _mm(jnp.tanh(zw), w2_ref[...]))) - 0.5
    a = jax.nn.sigmoid(a0_ref[...] + _mm(za, a2_ref[...]))
    r_ref[0] = r
    lw_ref[0] = -jnp.exp(w_log)
    kx_ref[0] = k * kk_ref[...]
    km_ref[0] = k * (1.0 + (a - 1.0) * ka_ref[...])
    v_ref[0] = v
    a_ref[0] = a
    go_ref[0] = _mm(jax.nn.sigmoid(zg), g2_ref[...])


def _pad_rows(w, n):
    return jnp.pad(w, ((0, n - w.shape[0]), (0, 0)))


def _rw_prep(x, norm1, w_in, mu, w0, w2, a0, a2, g2, k_k, k_a):
    B, S, D = x.shape
    W = MIX_WIDTH
    TM = min(256, S)
    o = np.cumsum([0, 3 * W, DECAY_LORA, ICLR_LORA, GATE_LORA, MEM_WIDTH])
    seg = [w_in[:, o[i]:o[i + 1]] for i in range(5)]
    padc = lambda w, n: jnp.pad(w, ((0, 0), (0, n - w.shape[1])))
    w_row = jnp.concatenate(
        [seg[0], padc(seg[1], LANES), padc(seg[2], LANES), padc(seg[3], 2 * LANES),
         seg[4] * HEAD_DIM ** -0.5], axis=1).astype(BF16)
    mus = [mu[o[i]:o[i + 1]] for i in range(4)]
    padv = lambda v, n: jnp.pad(v, (0, n - v.shape[0]))
    mu_p = jnp.concatenate([mus[0], padv(mus[1], LANES), padv(mus[2], LANES),
                            padv(mus[3], 2 * LANES)]).reshape(1, RW_Z_W)
    vec = lambda v: v.reshape(1, W)
    full = lambda a: pl.BlockSpec(a.shape, lambda b, s: (0,) * a.ndim)
    args = [norm1.reshape(1, D), w_row, mu_p, vec(w0), _pad_rows(w2, LANES).astype(BF16), vec(a0),
            _pad_rows(a2, LANES).astype(BF16), _pad_rows(g2, 2 * LANES).astype(BF16), vec(k_k), vec(k_a)]
    oblk = pl.BlockSpec((1, TM, W), lambda b, s: (b, s, 0))
    return pl.pallas_call(
        _rw_prep_kernel,
        grid=(B, S // TM),
        in_specs=[pl.BlockSpec((1, TM, D), lambda b, s: (b, s, 0))] + [full(a) for a in args],
        out_specs=[oblk] * 7 + [pl.BlockSpec((1, TM, MEM_WIDTH), lambda b, s: (b, s, 0))],
        out_shape=[jax.ShapeDtypeStruct((B, S, W), F32)] * 7
        + [jax.ShapeDtypeStruct((B, S, MEM_WIDTH), BF16)],
        scratch_shapes=[pltpu.VMEM((8, RW_Z_W), F32)],
        compiler_params=_cparams(("parallel", "arbitrary")),
        name="rw_prep",
    )(x, *args)


def _split2(x):
    hi = x.astype(BF16)
    return hi, (x - hi.astype(F32)).astype(BF16)


def _mm3(a, b):
    dot = functools.partial(jnp.dot, preferred_element_type=F32)
    return dot(a[0], b[0]) + dot(a[0], b[1]) + dot(a[1], b[0])


def _each(f, *lists):
    return [f(*args) for args in zip(*lists)]


def _tri_inverse(a2s, eye, same16, same32, same64):
    ds = _each(lambda a2: jnp.where(same16, a2, 0.0), a2s)
    xs = _each(lambda d: eye + d, ds)
    for _ in range(3):
        ds = _each(lambda d: _mm(d, d), ds)
        xs = _each(lambda x, d: x + _mm(x, d), xs, ds)
    for lo, hi_ in ((same16, same32), (same32, same64)):
        sel = hi_ & jnp.logical_not(lo)
        mids = _each(lambda x, a2: _mm(x, jnp.where(sel, a2, 0.0)), xs, a2s)
        xs = _each(lambda x, m: x + _mm(m, x), xs, mids)
    res = _each(lambda x, a2: (eye - x) + _mm3(_split2(a2), _split2(x)), xs, a2s)
    return _each(lambda x, r: x + _mm(x, r), xs, res)


def _rw_scan_kernel(r_ref, lw_ref, kx_ref, km_ref, v_ref, a_ref, g_ref, rk_ref, lnw_ref, lnb_ref,
                    o_ref, st_scr, *, TS, C):
    Dh = HEAD_DIM

    @pl.when(pl.program_id(1) == 0)
    def _():
        st_scr[...] = jnp.zeros(st_scr.shape, F32)

    head0 = _iota((1, LANES), 1) < Dh
    ltri = jnp.where(_iota((C, C), 1) <= _iota((C, C), 0), 1.0, 0.0)
    col2 = _iota((C, LANES), 1) & (Dh - 1)
    row2 = _iota((C, LANES), 0)
    m_incl = col2 <= row2
    m_strict = col2 < row2
    r128 = _iota((LANES, LANES), 0)
    c128 = _iota((LANES, LANES), 1)
    eye = jnp.where(r128 == c128, 1.0, 0.0)
    same16 = (r128 >> 4) == (c128 >> 4)
    same32 = (r128 >> 5) == (c128 >> 5)
    same64 = (r128 >> 6) == (c128 >> 6)
    zeros = jnp.zeros((C, LANES), F32)
    ltri = ltri.astype(BF16)

    def hsum(x):
        s0 = jnp.sum(jnp.where(head0, x, 0.0), axis=-1, keepdims=True)
        s1 = jnp.sum(jnp.where(head0, 0.0, x), axis=-1, keepdims=True)
        return jnp.where(head0, s0, s1)

    n_pairs = MIX_WIDTH // LANES
    N_AHEAD = 2
    lns = [slice(pi * LANES, (pi + 1) * LANES) for pi in range(n_pairs)]
    pis = list(range(n_pairs))

    def cumsum_decay(lw):
        l1 = lw.astype(BF16)
        l2 = (lw - l1.astype(F32)).astype(BF16)
        l3 = (lw - l1.astype(F32) - l2.astype(F32)).astype(BF16)
        cum3 = jnp.dot(ltri, jnp.concatenate([l1, l2, l3], axis=1), preferred_element_type=F32)
        return cum3[:, :LANES] + cum3[:, LANES:2 * LANES] + cum3[:, 2 * LANES:]

    def scaled(ln, cum, sl):
        r, lw, kx, km, a = r_ref[0, sl, ln], lw_ref[0, sl, ln], kx_ref[0, sl, ln], km_ref[0, sl, ln], a_ref[0, sl, ln]
        kk = kx / jnp.maximum(jnp.sqrt(hsum(kx * kx)), 1e-12)
        p_in = jnp.exp(cum)
        at = -kk * jnp.exp(cum - lw)
        rt = r * p_in
        p_inv = jnp.exp(-cum)
        bk = jnp.concatenate([kk * a * p_inv, km * p_inv], axis=0)
        at0, at1 = jnp.where(head0, at, 0.0), jnp.where(head0, 0.0, at)
        rt0, rt1 = jnp.where(head0, rt, 0.0), jnp.where(head0, 0.0, rt)
        lhs = jnp.concatenate([at0, at1, rt0, rt1], axis=0)
        return lhs, bk, rt, p_in[C - 1:C, :]

    def split_aa(aa):
        aa0 = jnp.where(m_strict, aa[0:C], 0.0)
        aa1 = pltpu.roll(jnp.where(m_strict, aa[C:2 * C], 0.0), Dh, 1)
        ar0 = jnp.where(m_incl, aa[2 * C:3 * C], 0.0)
        ar1 = jnp.where(m_incl, aa[3 * C:4 * C], 0.0)
        a2 = jnp.concatenate([jnp.where(head0, aa0, 0.0), jnp.where(head0, 0.0, aa1)], axis=0)
        return aa0, aa1, ar0, ar1, a2

    def epilogue(ln, y, sl):
        r, km, v = r_ref[0, sl, ln], km_ref[0, sl, ln], v_ref[0, sl, ln]
        mean = hsum(y) * (1.0 / Dh)
        d = y - mean
        var = hsum(d * d) * (1.0 / Dh)
        yn = d * lax.rsqrt(var + GN_EPS) * lnw_ref[:, ln] + lnb_ref[:, ln]
        bonus = hsum(r * km * rk_ref[:, ln]) * v
        o_ref[0, sl, ln] = ((yn + bonus) * g_ref[0, sl, ln]).astype(BF16)

    def chunk_group(cg, carry):
        sls = [pl.ds(pl.multiple_of((cg * N_AHEAD + j) * C, C), C) for j in range(N_AHEAD)]
        sl_i = [sl for sl in sls for _ in lns]
        ln_i = [ln for _ in sls for ln in lns]
        cums = _each(lambda sl, ln: cumsum_decay(lw_ref[0, sl, ln]), sl_i, ln_i)
        lhss, bks, rts, pcs = zip(*_each(scaled, ln_i, cums, sl_i))
        aas = _each(lambda lhs, bk: _mm(lhs, bk, _NT), lhss, bks)
        aa0s, aa1s, ar0s, ar1s, a2s = zip(*_each(split_aa, aas))
        t2s = _tri_inverse(a2s, eye, same16, same32, same64)
        vs = _each(lambda sl, ln: v_ref[0, sl, ln], sl_i, ln_i)
        x0s = _each(lambda aa0, v: _mm(aa0, jnp.concatenate([zeros, v], axis=0)), aa0s, vs)
        x1s = _each(lambda aa1, v: _mm(aa1, jnp.concatenate([v, zeros], axis=0)), aa1s, vs)
        wus = _each(lambda t2, lhs, x0, x1: _mm(t2, jnp.concatenate(
            [lhs[:2 * C], jnp.concatenate([x0, x1], axis=0)], axis=1)), t2s, lhss, x0s, x1s)
        for j, sl in enumerate(sls):
            k = slice(j * n_pairs, (j + 1) * n_pairs)
            sts = _each(lambda pi: st_scr[pi], pis)
            us = _each(lambda wu, st: _mm(wu[:C, :LANES] + wu[C:, :LANES], st, _NT)
                       + jnp.where(head0, wu[:C, LANES:], wu[C:, LANES:]), wus[k], sts)
            uvs = _each(lambda u, v: jnp.concatenate([u, v], axis=0), us, vs[k])
            ys = _each(lambda rt, st, ar0, ar1, uv: _mm(rt, st, _NT)
                       + jnp.where(head0, _mm(ar0, uv), _mm(ar1, uv)), rts[k], sts, ar0s[k], ar1s[k], uvs)
            new = _each(lambda st, pc, uv, bk: st * pc + jnp.where(same64, _mm(uv, bk * pc, _TN), 0.0),
                        sts, pcs[k], uvs, bks[k])
            for pi in pis:
                st_scr[pi] = new[pi]
            _each(lambda ln, y: epilogue(ln, y, sl), lns, ys)
        return carry

    lax.fori_loop(0, TS // (C * N_AHEAD), chunk_group, 0)


def _rw_scan(r, lw, kx, km, v, a, g, r_k, lnx_w, lnx_b):
    B, S, W = r.shape
    TS = min(256, S)
    C = 64
    blk = pl.BlockSpec((1, TS, W), lambda b, s: (b, s, 0))
    vblk = pl.BlockSpec((1, W), lambda b, s: (0, 0))
    kern = functools.partial(_rw_scan_kernel, TS=TS, C=C)
    return pl.pallas_call(
        kern,
        grid=(B, S // TS),
        in_specs=[blk] * 7 + [vblk] * 3,
        out_specs=blk,
        out_shape=jax.ShapeDtypeStruct((B, S, W), BF16),
        scratch_shapes=[pltpu.VMEM((W // LANES, LANES, LANES), F32)],
        compiler_params=_cparams(("parallel", "arbitrary")),
        name="rw_scan",
    )(r, lw, kx, km, v, a, g, r_k.reshape(1, W), lnx_w.reshape(1, W), lnx_b.reshape(1, W))


def kernel(x, mem, norm1, norm_mem, w_mem_kv, w_o, norm2, w_ffn_in, w_ffn_out, nsa_w_in, nsa_gate_b,
           nsa_cmp_pos, nsa_cmp_w1, nsa_cmp_w2, rw_w_in, rw_mu, rw_w0, rw_w2, rw_a0, rw_a2, rw_g2,
           rw_k_k, rw_k_a, rw_r_k, rw_lnx_w, rw_lnx_b, final_norm):
    depth = norm1.shape[0]
    B, S, _ = x.shape
    ktm, vm = _mem_kv(mem, norm_mem, w_mem_kv)
    for i in range(depth):
        j = i // 2
        if i % 2 == 0:
            q, kc, vc, vs, vw, kst, kwt, qm, gates = _nsa_proj(x, norm1[i], nsa_w_in[j], nsa_gate_b[j])
            kcb, vcb = _compress(kc, vc, nsa_cmp_pos[j], nsa_cmp_w1[j], nsa_cmp_w2[j])
            mix = _nsa_attn(q, kcb, vcb, kst, vs, kwt, vw, gates)
        else:
            r, lw, kx, km, v, a, g, qm = _rw_prep(x, norm1[i], rw_w_in[j], rw_mu[j], rw_w0[j], rw_w2[j],
                                                  rw_a0[j], rw_a2[j], rw_g2[j], rw_k_k[j], rw_k_a[j])
            mix = _rw_scan(r, lw, kx, km, v, a, g, rw_r_k[j], rw_lnx_w[j], rw_lnx_b[j])
            mix = mix.reshape(B, 1, S, MIX_WIDTH)
        cross = _mem_attn(qm, ktm[i], vm[i])
        x = _post(x, mix, cross, w_o[i], norm2[i], w_ffn_in[i], w_ffn_out[i], final_norm,
                  final=(i == depth - 1))
    return x
```

```python
import functools
import math

import numpy as np
import jax
import jax.numpy as jnp
from jax import lax
from jax.experimental import pallas as pl
from jax.experimental.pallas import tpu as pltpu

F32 = jnp.float32
BF16 = jnp.bfloat16
HI = lax.Precision.HIGHEST

D_MODEL = 1024
HEAD_DIM = 64
MIX_WIDTH = 768
MIX_HEADS = 12
MEM_HEADS = 4
MEM_WIDTH = 256
KV_GROUPS = 4
GROUP_SIZE = 3
KV_WIDTH = 256
CMP_BLOCK = 32
CMP_STRIDE = 16
CMP_HIDDEN = 128
SEL_BLOCK = 64
SEL_TOPN = 16
WINDOW = 512
FORCE_SCORE = 1.0e4
DECAY_LORA = 64
ICLR_LORA = 64
GATE_LORA = 160
GN_EPS = 64e-5
FFN_HIDDEN = 2816
RMS_EPS = 1e-6
NEG_INF = -1e30
LOG2E = 1.4426950408889634

LANES = 128
VMEM_LIMIT = 56 * 1024 * 1024

_NT = (((1,), (1,)), ((), ()))
_TN = (((0,), (0,)), ((), ()))


def _mm(a, b, dims=None, hi=False):
    if hi:
        a, b, prec = a.astype(F32), b.astype(F32), HI
    else:
        a, b, prec = a.astype(BF16), b.astype(BF16), None
    if dims is None:
        return jnp.dot(a, b, preferred_element_type=F32, precision=prec)
    return lax.dot_general(a, b, dims, preferred_element_type=F32, precision=prec)


def _iota(shape, dim):
    return lax.broadcasted_iota(jnp.int32, shape, dim)


def _rms(x, g):
    ms = jnp.mean(x * x, axis=-1, keepdims=True)
    return x * lax.rsqrt(ms + RMS_EPS) * g


def _cparams(sem):
    return pltpu.CompilerParams(dimension_semantics=sem, vmem_limit_bytes=VMEM_LIMIT)


def _mem_kv_kernel(mem_ref, g_ref, wkt_ref, wv_ref, kt_ref, v_ref):
    mn = _rms(mem_ref[0], g_ref[0]).astype(BF16)
    kt = _mm(wkt_ref[0], mn, _NT)
    v = _mm(mn, wv_ref[0])
    rowh = _iota(kt.shape, 0) // HEAD_DIM
    colh = _iota(v.shape, 1) // HEAD_DIM
    for h in range(MEM_HEADS):
        kt_ref[0, 0, h] = jnp.where(rowh == h, kt, 0.0).astype(BF16)
        v_ref[0, 0, h] = jnp.where(colh == h, v, 0.0).astype(BF16)


def _mem_kv(mem, norm_mem, w_mem_kv):
    B, M, D = mem.shape
    L = norm_mem.shape[0]
    wkt = jnp.swapaxes(w_mem_kv[:, :, :MEM_WIDTH], 1, 2).astype(BF16)
    wv = w_mem_kv[:, :, MEM_WIDTH:].astype(BF16)
    return pl.pallas_call(
        _mem_kv_kernel,
        grid=(L, B),
        in_specs=[
            pl.BlockSpec((1, M, D), lambda l, b: (b, 0, 0)),
            pl.BlockSpec((1, 1, D), lambda l, b: (l, 0, 0)),
            pl.BlockSpec((1, MEM_WIDTH, D), lambda l, b: (l, 0, 0)),
            pl.BlockSpec((1, D, MEM_WIDTH), lambda l, b: (l, 0, 0)),
        ],
        out_specs=[
            pl.BlockSpec((1, 1, MEM_HEADS, MEM_WIDTH, M), lambda l, b: (l, b, 0, 0, 0)),
            pl.BlockSpec((1, 1, MEM_HEADS, M, MEM_WIDTH), lambda l, b: (l, b, 0, 0, 0)),
        ],
        out_shape=[
            jax.ShapeDtypeStruct((L, B, MEM_HEADS, MEM_WIDTH, M), BF16),
            jax.ShapeDtypeStruct((L, B, MEM_HEADS, M, MEM_WIDTH), BF16),
        ],
        compiler_params=_cparams(("parallel", "parallel")),
        name="mem_kv",
    )(mem, norm_mem.reshape(L, 1, D), wkt, wv)


def _post_kernel(x_ref, mix_ref, qm_ref, kt_ref, v_ref, wo_ref, g2_ref, wg_ref, wu_ref, wout_ref,
                 gf_ref, o_ref, x1_scr, hn_scr, acc_scr, *, n_h, final):
    h = pl.program_id(2)

    @pl.when(h == 0)
    def _():
        q = qm_ref[0]
        cross = jnp.zeros(q.shape, F32)
        for hd in range(MEM_HEADS):
            s = _mm(q, kt_ref[0, hd])
            p = jnp.exp(s - jnp.max(s, axis=-1, keepdims=True))
            l = jnp.sum(p, axis=-1, keepdims=True)
            cross = cross + _mm(p * (1.0 / l), v_ref[0, hd])
        x1 = x_ref[0] + _mm(mix_ref[0], wo_ref[:MIX_WIDTH]) + _mm(cross, wo_ref[MIX_WIDTH:])
        x1_scr[...] = x1
        hn_scr[...] = _rms(x1, g2_ref[...]).astype(BF16)
        acc_scr[...] = jnp.zeros(acc_scr.shape, F32)

    hn = hn_scr[...]
    gate = _mm(hn, wg_ref[...])
    up = _mm(hn, wu_ref[...])
    hid = gate * jax.nn.sigmoid(gate) * up
    acc_scr[...] += _mm(hid, wout_ref[...])

    @pl.when(h == n_h - 1)
    def _():
        y = x1_scr[...] + acc_scr[...]
        if final:
            y = _rms(y, gf_ref[...])
        o_ref[0] = y


def _post(x, mix, qm, ktm, vm, w_o, norm2, w_ffn_in, w_ffn_out, final_norm, final):
    B, S, D = x.shape
    M = ktm.shape[-1]
    TM = min(1024, S)
    TH = 256
    NH = FFN_HIDDEN // TH
    wi = w_ffn_in.astype(BF16)
    kern = functools.partial(_post_kernel, n_h=NH, final=final)
    return pl.pallas_call(
        kern,
        grid=(B, S // TM, NH),
        in_specs=[
            pl.BlockSpec((1, TM, D), lambda b, s, h: (b, s, 0)),
            pl.BlockSpec((1, TM, MIX_WIDTH), lambda b, s, h: (b, s, 0)),
            pl.BlockSpec((1, TM, MEM_WIDTH), lambda b, s, h: (b, s, 0)),
            pl.BlockSpec((1, MEM_HEADS, MEM_WIDTH, M), lambda b, s, h: (b, 0, 0, 0)),
            pl.BlockSpec((1, MEM_HEADS, M, MEM_WIDTH), lambda b, s, h: (b, 0, 0, 0)),
            pl.BlockSpec((MIX_WIDTH + MEM_WIDTH, D), lambda b, s, h: (0, 0)),
            pl.BlockSpec((1, D), lambda b, s, h: (0, 0)),
            pl.BlockSpec((D, TH), lambda b, s, h: (0, h)),
            pl.BlockSpec((D, TH), lambda b, s, h: (0, NH + h)),
            pl.BlockSpec((TH, D), lambda b, s, h: (h, 0)),
            pl.BlockSpec((1, D), lambda b, s, h: (0, 0)),
        ],
        out_specs=pl.BlockSpec((1, TM, D), lambda b, s, h: (b, s, 0)),
        out_shape=jax.ShapeDtypeStruct((B, S, D), F32),
        scratch_shapes=[
            pltpu.VMEM((TM, D), F32),
            pltpu.VMEM((TM, D), BF16),
            pltpu.VMEM((TM, D), F32),
        ],
        compiler_params=_cparams(("parallel", "parallel", "arbitrary")),
        name="post_ffn",
    )(x, mix, qm, ktm, vm, w_o.astype(BF16), norm2.reshape(1, D), wi, wi, w_ffn_out.astype(BF16),
      final_norm.reshape(1, D))


NSA_ROW_W = MIX_WIDTH + 4 * KV_WIDTH + MEM_WIDTH + LANES


def _nsa_proj_kernel(x_ref, g_ref, w_ref, wt_ref, gb_ref, qc_ref, q_ref, kc_ref, vc_ref, vs_ref, vw_ref,
                     kst_ref, kwt_ref, qm_ref, gate_ref):
    hn = _rms(x_ref[0], g_ref[...]).astype(BF16)
    res = _mm(hn, w_ref[...])
    tm = res.shape[0]
    for h in range(MIX_HEADS):
        q_ref[0, h] = jnp.concatenate(
            [res[:, h * HEAD_DIM:(h + 1) * HEAD_DIM], jnp.broadcast_to(qc_ref[h], (tm, HEAD_DIM))],
            axis=1).astype(BF16)
    ones_col = jnp.ones((tm, HEAD_DIM), F32)
    off = MIX_WIDTH
    for ref, with_ones in ((kc_ref, False), (vc_ref, False), (vs_ref, True), (vw_ref, True)):
        for g in range(KV_GROUPS):
            t = res[:, off + g * HEAD_DIM: off + (g + 1) * HEAD_DIM]
            if with_ones:
                t = jnp.concatenate([t, ones_col], axis=1)
            ref[0, g] = t.astype(BF16)
        off += KV_WIDTH
    qm_ref[0] = res[:, off:off + MEM_WIDTH].astype(BF16)
    off += MEM_WIDTH
    gates = jax.nn.sigmoid(res[:, off:off + 64] + gb_ref[...])
    for g in range(KV_GROUPS):
        gate_ref[0, g] = gates[:, g * 16:(g + 1) * 16]
    rt = _mm(wt_ref[...], hn, _NT)
    kst_ref[0] = rt[:KV_WIDTH].astype(BF16)
    kwt_ref[0] = rt[KV_WIDTH:].astype(BF16)


def _alibi_slopes(n):
    def pow2(m):
        start = 2.0 ** (-8.0 / m)
        return [start ** (i + 1) for i in range(m)]
    c = 2 ** int(math.floor(math.log2(n)))
    s = pow2(c)
    if c < n:
        s = s + pow2(2 * c)[0::2][: n - c]
    return np.asarray(s, dtype=np.float32)


def _pos_pieces(pos):
    a64 = (pos >> 6) * 64
    b = pos & 63
    return np.stack([a64, a64, a64, b, b, b]).astype(np.float32)


def _slope_pieces():
    sl = jnp.asarray(_alibi_slopes(MIX_HEADS) * np.float32(LOG2E), F32)
    s1 = sl.astype(BF16).astype(F32)
    s2 = (sl - s1).astype(BF16).astype(F32)
    s3 = (sl - s1 - s2).astype(BF16).astype(F32)
    six = jnp.stack([s1, s2, s3, s1, s2, s3], axis=1)
    return jnp.pad(six, ((0, 0), (0, HEAD_DIM - 6))).reshape(MIX_HEADS, 1, HEAD_DIM)


def _nsa_proj(x, norm1, w_in, gate_b):
    B, S, D = x.shape
    TM = min(512, S)
    G, H, Dh = KV_GROUPS, MIX_HEADS, HEAD_DIM
    scale = HEAD_DIM ** -0.5
    o = np.cumsum([0, MIX_WIDTH] + [KV_WIDTH] * 6 + [3 * MIX_HEADS, MEM_WIDTH])
    wq, wkc, wvc, wks, wvs, wkw, wvw, wgl, wqm = (w_in[:, o[i]:o[i + 1]] for i in range(9))
    wgl = wgl.reshape(D, 3, G, GROUP_SIZE).transpose(0, 2, 1, 3).reshape(D, G, 9)
    wgl = jnp.pad(wgl, ((0, 0), (0, 0), (0, 7))).reshape(D, 64)
    gb = gate_b.reshape(3, G, GROUP_SIZE).transpose(1, 0, 2).reshape(G, 9)
    gb = jnp.pad(gb, ((0, 0), (0, 7))).reshape(1, 64)
    w_row = jnp.concatenate(
        [wq * (scale * LOG2E), wkc, wvc, wvs, wvw, wqm * scale, wgl, jnp.zeros((D, LANES - 64), F32)],
        axis=1).astype(BF16)
    w_t = jnp.concatenate([wks, wkw], axis=1).T.astype(BF16)
    head = lambda n, w: pl.BlockSpec((1, n, TM, w), lambda b, s: (b, 0, s, 0))
    return pl.pallas_call(
        _nsa_proj_kernel,
        grid=(B, S // TM),
        in_specs=[
            pl.BlockSpec((1, TM, D), lambda b, s: (b, s, 0)),
            pl.BlockSpec((1, D), lambda b, s: (0, 0)),
            pl.BlockSpec((D, NSA_ROW_W), lambda b, s: (0, 0)),
            pl.BlockSpec((2 * KV_WIDTH, D), lambda b, s: (0, 0)),
            pl.BlockSpec((1, 64), lambda b, s: (0, 0)),
            pl.BlockSpec((H, 1, Dh), lambda b, s: (0, 0, 0)),
        ],
        out_specs=[
            head(H, LANES), head(G, Dh), head(G, Dh), head(G, LANES), head(G, LANES),
            pl.BlockSpec((1, KV_WIDTH, TM), lambda b, s: (b, 0, s)),
            pl.BlockSpec((1, KV_WIDTH, TM), lambda b, s: (b, 0, s)),
            pl.BlockSpec((1, TM, MEM_WIDTH), lambda b, s: (b, s, 0)),
            pl.BlockSpec((1, G, TM, 16), lambda b, s: (b, 0, s, 0)),
        ],
        out_shape=[
            jax.ShapeDtypeStruct((B, H, S, LANES), BF16),
            jax.ShapeDtypeStruct((B, G, S, Dh), BF16),
            jax.ShapeDtypeStruct((B, G, S, Dh), BF16),
            jax.ShapeDtypeStruct((B, G, S, LANES), BF16),
            jax.ShapeDtypeStruct((B, G, S, LANES), BF16),
            jax.ShapeDtypeStruct((B, KV_WIDTH, S), BF16),
            jax.ShapeDtypeStruct((B, KV_WIDTH, S), BF16),
            jax.ShapeDtypeStruct((B, S, MEM_WIDTH), BF16),
            jax.ShapeDtypeStruct((B, G, S, 16), F32),
        ],
        compiler_params=_cparams(("parallel", "parallel")),
        name="nsa_proj",
    )(x, norm1.reshape(1, D), w_row, w_t, gb, _slope_pieces())


def _compress_kernel(kc_ref, vc_ref, w1_ref, w2_ref, pos_ref, cpos_ref, kcb_ref, vcb_ref):
    half = CMP_STRIDE * HEAD_DIM
    for idx, (src, dst) in enumerate(((kc_ref, kcb_ref), (vc_ref, vcb_ref))):
        c = src[0, 0]
        ncp = c.shape[0]
        a = _mm(c, w1_ref[idx, :half])
        bm = _mm(c, w1_ref[idx, half:])
        bias = _mm(pos_ref[idx], w1_ref[idx])[0:1]
        hid = jax.nn.gelu(a + pltpu.roll(bm, ncp - 1, 0) + bias)
        out = _mm(hid, w2_ref[idx])
        out = jnp.where(_iota(out.shape, 0) < ncp - 1, out, 0.0)
        pad = cpos_ref[...] if idx == 0 else jnp.zeros(out.shape, F32)
        out = jnp.concatenate([out, pad], axis=1)
        dst[0, 0] = out.astype(BF16)


def _compress(kc, vc, cmp_pos, cmp_w1, cmp_w2):
    B, G, S, Dh = kc.shape
    NCP = S // CMP_STRIDE
    kcr = kc.reshape(B, G, NCP, CMP_STRIDE * Dh)
    vcr = vc.reshape(B, G, NCP, CMP_STRIDE * Dh)
    w1 = cmp_w1.reshape(2, CMP_BLOCK * Dh, CMP_HIDDEN).astype(BF16)
    w2 = cmp_w2.astype(BF16)
    pos = jnp.broadcast_to(cmp_pos.reshape(2, 1, CMP_BLOCK * Dh), (2, 8, CMP_BLOCK * Dh)).astype(BF16)
    cend = np.arange(NCP) * CMP_STRIDE + CMP_BLOCK - 1
    cpos = np.zeros((NCP, Dh), np.float32)
    cpos[:, :6] = _pos_pieces(cend).T
    blk = pl.BlockSpec((1, 1, NCP, CMP_STRIDE * Dh), lambda b, g: (b, g, 0, 0))
    return pl.pallas_call(
        _compress_kernel,
        grid=(B, G),
        in_specs=[
            blk, blk,
            pl.BlockSpec((2, CMP_BLOCK * Dh, CMP_HIDDEN), lambda b, g: (0, 0, 0)),
            pl.BlockSpec((2, CMP_HIDDEN, Dh), lambda b, g: (0, 0, 0)),
            pl.BlockSpec((2, 8, CMP_BLOCK * Dh), lambda b, g: (0, 0, 0)),
            pl.BlockSpec((NCP, Dh), lambda b, g: (0, 0)),
        ],
        out_specs=[pl.BlockSpec((1, 1, NCP, LANES), lambda b, g: (b, g, 0, 0))] * 2,
        out_shape=[jax.ShapeDtypeStruct((B, G, NCP, LANES), BF16)] * 2,
        compiler_params=_cparams(("parallel", "parallel")),
        name="nsa_compress",
    )(kcr, vcr, w1, w2, pos, jnp.asarray(cpos))


def _exp2_bf16(s, m):
    return jnp.concatenate(
        [jnp.exp2((s[:, i * LANES:(i + 1) * LANES] - m).astype(BF16)) for i in range(s.shape[1] // LANES)],
        axis=1)


def _nsa_attn_kernel(q_ref, kcb_ref, vcb_ref, ovt_ref, kst_ref, ksc_ref, vs_ref, kwt_ref, kwc_ref,
                     vw_ref, gate_ref, eg_ref, o_ref, score_scr, m_scr, acc_scr, s_scr, sw_scr, *, TQ, TK, S, n_sel, NG):
    R, Dh, CH = GROUP_SIZE, HEAD_DIM, LANES
    t0 = pl.program_id(2) * TQ
    NCP, NB = S // CMP_STRIDE, S // SEL_BLOCK
    n_sub = TQ // CH
    n_ch = R * n_sub
    groups = list(range(NG))
    chunks = [(g, c) for g in groups for c in range(n_ch)]
    rows_of = lambda c: slice(c * CH, (c + 1) * CH)
    qa = [q_ref[0, g * R:(g + 1) * R].reshape(R * TQ, LANES) for g in groups]
    tq_col = t0 + _iota((TQ, 1), 0)
    tq_col3 = t0 + (_iota((R * TQ, 1), 0) & (TQ - 1))

    sc = [_mm(qa[g], kcb_ref[0, g], _NT) for g in groups]
    WK = WINDOW + TQ
    w0 = pl.multiple_of(jnp.maximum(t0 - WINDOW, 0), LANES)
    consts_w = kwc_ref[:, pl.ds(w0, WK)]
    for g in groups:
        sw_scr[g] = jnp.dot(qa[g], jnp.concatenate(
            [kwt_ref[0, g * Dh:(g + 1) * Dh, pl.ds(w0, WK)], consts_w], axis=0), preferred_element_type=F32)
    n_row = _iota((1, NCP), 1)
    cend = n_row * CMP_STRIDE + (CMP_BLOCK - 1)
    mask_c = (cend <= tq_col) & (n_row < NCP - 1)
    psum = [[jnp.zeros((CH, NCP), F32) for _ in range(n_sub)] for _ in groups]
    o_c = []
    for g, c in chunks:
        mk = mask_c[(c % n_sub) * CH:(c % n_sub + 1) * CH]
        s = jnp.where(mk, sc[g][rows_of(c)], NEG_INF)
        m = jnp.max(s, axis=-1, keepdims=True)
        p = jnp.where(mk, jnp.exp2(s - m), 0.0)
        l = jnp.sum(p, axis=-1, keepdims=True)
        p = p * jnp.where(l > 0.0, 1.0 / l, 0.0)
        psum[g][c % n_sub] = psum[g][c % n_sub] + p
        o_c.append(_mm(p, vcb_ref[0, g]))
    psum = [jnp.concatenate(psum[g], axis=0) for g in groups]
    o_c = [jnp.concatenate(o_c[g * n_ch:(g + 1) * n_ch], axis=0) for g in groups]

    tb = lax.shift_right_logical(t0 + _iota((1, TQ), 1), 6)
    j_col = _iota((NB, 1), 0)
    valid = j_col <= tb
    forced = (j_col == 0) | (j_col == tb) | (j_col == tb - 1)
    score = []
    for g in groups:
        imp_t = _mm(ovt_ref[...], psum[g], _NT, hi=True)
        score.append(jnp.where(valid, jnp.where(forced, FORCE_SCORE, imp_t), -jnp.inf))
        score_scr[g] = score[g]
    n_valid = (t0 + TQ - 1) // SEL_BLOCK + 1

    def rank_body(i, cnts):
        tie = jnp.where(i < j_col, 1.0, 0.0)
        out = []
        for g in groups:
            row = score_scr[g, pl.ds(i, 1), :]
            out.append(cnts[g] + jnp.where(row > score[g], 1.0, jnp.where(row == score[g], tie, 0.0)))
        return tuple(out)

    rank = lax.fori_loop(0, n_valid, rank_body, tuple(jnp.zeros((NB, TQ), F32) for _ in groups))
    q2 = []
    for g in groups:
        unsel_t = jnp.where(valid, jnp.where(rank[g] < n_sel, 0.0, 1.0), 1.0)
        unsel = jnp.concatenate([unsel_t, jnp.zeros((LANES - NB, TQ), F32)], axis=0).T.astype(BF16)
        q2.append(jnp.concatenate([qa[g], jnp.concatenate([unsel] * R, axis=0)], axis=1))

    def scores_to(slot, kt):
        k0 = pl.multiple_of(kt * TK, TK)
        consts = ksc_ref[:, pl.ds(k0, TK)]
        for g in groups:
            s_scr[slot, g] = jnp.dot(q2[g], jnp.concatenate(
                [kst_ref[0, g * Dh:(g + 1) * Dh, pl.ds(k0, TK)], consts], axis=0), preferred_element_type=F32)

    scores_to(0, 0)

    kpos_w = w0 + _iota((1, WK), 1)
    mask_w = []
    for h in range(n_sub):
        dist = (t0 + h * CH + _iota((CH, 1), 0)) - kpos_w
        mask_w.append((dist >= 0) & (dist < WINDOW))
    ow = []
    for g, c in chunks:
        s = jnp.where(mask_w[c % n_sub], sw_scr[g, rows_of(c), :], NEG_INF)
        m = jnp.broadcast_to(jnp.max(s, axis=-1, keepdims=True), (CH, LANES))
        ow.append(jnp.dot(_exp2_bf16(s, m), vw_ref[0, g, pl.ds(w0, WK), :], preferred_element_type=F32))
    ow = [jnp.concatenate(ow[g * n_ch:(g + 1) * n_ch], axis=0) for g in groups]

    m_scr[...] = jnp.full(m_scr.shape, NEG_INF, F32)
    acc_scr[...] = jnp.zeros(acc_scr.shape, F32)
    n_kt = (t0 + TQ - 1) // TK + 1

    def accumulate(kt, slot, causal):
        k0 = pl.multiple_of(kt * TK, TK)

        def tile(g, c):
            x = s_scr[slot, g, rows_of(c), :]
            if causal:
                keep = k0 + _iota((1, TK), 1) <= t0 + (c % n_sub) * CH + _iota((CH, 1), 0)
                x = jnp.where(keep, x, NEG_INF)
            return x

        m_old = [m_scr[g, rows_of(c)] for g, c in chunks]
        m_new = [jnp.maximum(mo, jnp.max(tile(g, c), axis=-1, keepdims=True))
                 for mo, (g, c) in zip(m_old, chunks)]
        for i, (g, c) in enumerate(chunks):
            p = _exp2_bf16(tile(g, c), m_new[i])
            pv = jnp.dot(p, vs_ref[0, g, pl.ds(k0, TK), :], preferred_element_type=F32)
            acc_scr[g, rows_of(c)] = jnp.exp2(m_old[i] - m_new[i]) * acc_scr[g, rows_of(c)] + pv
            m_scr[g, rows_of(c)] = m_new[i]

    n_full = n_kt - 1

    def pair_body(j, carry):
        scores_to(1, 2 * j + 1)
        accumulate(2 * j, 0, False)
        scores_to(0, 2 * j + 2)
        accumulate(2 * j + 1, 1, False)
        return carry

    lax.fori_loop(0, n_full // 2, pair_body, 0)

    @pl.when(n_full % 2 == 1)
    def _():
        scores_to(1, n_full)
        accumulate(n_full - 1, 0, False)
        accumulate(n_full, 1, True)

    @pl.when(n_full % 2 == 0)
    def _():
        accumulate(n_full, 0, True)

    for g in groups:
        acc = acc_scr[g]
        o_s = acc * (1.0 / pltpu.roll(acc, Dh, 1))
        o_w = ow[g] * (1.0 / pltpu.roll(ow[g], Dh, 1))
        gv = gate_ref[0, g]
        g_hi = gv.astype(BF16)
        g_lo = (gv - g_hi.astype(F32)).astype(BF16)
        ge = (jnp.dot(g_hi, eg_ref[...], preferred_element_type=F32)
              + jnp.dot(g_lo, eg_ref[...], preferred_element_type=F32))
        gate = lambda k: ge[:, k * LANES:(k + 1) * LANES]
        for r in range(R):
            rows = slice(r * TQ, (r + 1) * TQ)
            out = gate(r) * o_c[g][rows] + gate(R + r) * o_s[rows] + gate(2 * R + r) * o_w[rows]
            o_ref[0, :, (g * R + r) * Dh:(g * R + r + 1) * Dh] = out[:, :Dh].astype(BF16)


def _nsa_attn(q, kcb, vcb, kst, vs, kwt, vw, gates):
    B, H, S, _ = q.shape
    G, R, Dh = KV_GROUPS, GROUP_SIZE, HEAD_DIM
    TQ = 256
    TK = min(512, S)
    NG = 2
    NCP, NB = S // CMP_STRIDE, S // SEL_BLOCK
    n_sel = min(SEL_TOPN, NB)
    assert S % TK == 0 and S >= WINDOW + TQ and NB <= Dh and S <= 4096
    cs = np.arange(NCP) * CMP_STRIDE
    ss = np.arange(NB) * SEL_BLOCK
    ov = (cs[:, None] <= ss[None, :] + SEL_BLOCK - 1) & (cs[:, None] + CMP_BLOCK - 1 >= ss[None, :])
    ov[NCP - 1] = False
    ovt = jnp.asarray(ov.T.astype(np.float32))
    pieces = _pos_pieces(np.arange(S))
    ksc = np.zeros((3 * Dh, S), np.float32)
    ksc[:6] = pieces
    ksc[Dh:Dh + NB] = np.where(np.arange(S)[None, :] // SEL_BLOCK == np.arange(NB)[:, None], NEG_INF, 0.0)
    kwc = np.zeros((Dh, S), np.float32)
    kwc[:6] = pieces
    eg = (np.arange(9 * LANES)[None, :] // LANES == np.arange(16)[:, None]).astype(np.float32)
    kern = functools.partial(_nsa_attn_kernel, TQ=TQ, TK=TK, S=S, n_sel=n_sel, NG=NG)
    return pl.pallas_call(
        kern,
        grid=(B, G // NG, S // TQ),
        in_specs=[
            pl.BlockSpec((1, NG * R, TQ, LANES), lambda b, g, i: (b, g, i, 0)),
            pl.BlockSpec((1, NG, NCP, LANES), lambda b, g, i: (b, g, 0, 0)),
            pl.BlockSpec((1, NG, NCP, LANES), lambda b, g, i: (b, g, 0, 0)),
            pl.BlockSpec((NB, NCP), lambda b, g, i: (0, 0)),
            pl.BlockSpec((1, NG * Dh, S), lambda b, g, i: (b, g, 0)),
            pl.BlockSpec((3 * Dh, S), lambda b, g, i: (0, 0)),
            pl.BlockSpec((1, NG, S, LANES), lambda b, g, i: (b, g, 0, 0)),
            pl.BlockSpec((1, NG * Dh, S), lambda b, g, i: (b, g, 0)),
            pl.BlockSpec((Dh, S), lambda b, g, i: (0, 0)),
            pl.BlockSpec((1, NG, S, LANES), lambda b, g, i: (b, g, 0, 0)),
            pl.BlockSpec((1, NG, TQ, 16), lambda b, g, i: (b, g, i, 0)),
            pl.BlockSpec((16, 9 * LANES), lambda b, g, i: (0, 0)),
        ],
        out_specs=pl.BlockSpec((1, TQ, NG * R * Dh), lambda b, g, i: (b, i, g)),
        out_shape=jax.ShapeDtypeStruct((B, S, G * R * Dh), BF16),
        scratch_shapes=[
            pltpu.VMEM((NG, NB, TQ), F32),
            pltpu.VMEM((NG, R * TQ, LANES), F32),
            pltpu.VMEM((NG, R * TQ, LANES), F32),
            pltpu.VMEM((2, NG, R * TQ, TK), F32),
            pltpu.VMEM((NG, R * TQ, WINDOW + TQ), F32),
        ],
        compiler_params=_cparams(("parallel", "parallel", "arbitrary")),
        name="nsa_attn",
    )(q, kcb, vcb, ovt, kst, jnp.asarray(ksc, BF16), vs, kwt, jnp.asarray(kwc, BF16), vw, gates,
      jnp.asarray(eg, BF16))


RW_Z_W = 3 * MIX_WIDTH + 2 * LANES + 2 * LANES
RW_ROW_W = RW_Z_W + MEM_WIDTH


def _rw_prep_kernel(x_ref, g_ref, w_ref, mu_ref, w0_ref, w2_ref, a0_ref, a2_ref, g2_ref, kk_ref,
                    ka_ref, r_ref, lw_ref, kx_ref, km_ref, v_ref, a_ref, go_ref, qm_ref, carry_scr):
    W = MIX_WIDTH

    @pl.when(pl.program_id(1) == 0)
    def _():
        carry_scr[...] = jnp.zeros(carry_scr.shape, F32)

    hn = _rms(x_ref[0], g_ref[...]).astype(BF16)
    res = _mm(hn, w_ref[...])
    qm_ref[0] = res[:, RW_Z_W:].astype(BF16)
    z = res[:, :RW_Z_W]
    tm = z.shape[0]
    zprev = jnp.where(_iota((tm, 1), 0) == 0, carry_scr[0:1, :], pltpu.roll(z, 1, 0))
    carry_scr[0:1, :] = z[tm - 1:tm, :]
    z = z + (zprev - z) * mu_ref[...]
    r, k, v = z[:, :W], z[:, W:2 * W], z[:, 2 * W:3 * W]
    zw = z[:, 3 * W:3 * W + LANES]
    za = z[:, 3 * W + LANES:3 * W + 2 * LANES]
    zg = z[:, 3 * W + 2 * LANES:]
    w_log = -jax.nn.softplus(-(w0_ref[...] + _mm(jnp.tanh(zw), w2_ref[...]))) - 0.5
    a = jax.nn.sigmoid(a0_ref[...] + _mm(za, a2_ref[...]))
    r_ref[0] = r
    lw_ref[0] = -jnp.exp(w_log)
    kx_ref[0] = k * kk_ref[...]
    km_ref[0] = k * (1.0 + (a - 1.0) * ka_ref[...])
    v_ref[0] = v
    a_ref[0] = a
    go_ref[0] = _mm(jax.nn.sigmoid(zg), g2_ref[...])


def _pad_rows(w, n):
    return jnp.pad(w, ((0, n - w.shape[0]), (0, 0)))


def _rw_prep(x, norm1, w_in, mu, w0, w2, a0, a2, g2, k_k, k_a):
    B, S, D = x.shape
    W = MIX_WIDTH
    TM = min(256, S)
    o = np.cumsum([0, 3 * W, DECAY_LORA, ICLR_LORA, GATE_LORA, MEM_WIDTH])
    seg = [w_in[:, o[i]:o[i + 1]] for i in range(5)]
    padc = lambda w, n: jnp.pad(w, ((0, 0), (0, n - w.shape[1])))
    w_row = jnp.concatenate(
        [seg[0], padc(seg[1], LANES), padc(seg[2], LANES), padc(seg[3], 2 * LANES),
         seg[4] * HEAD_DIM ** -0.5], axis=1).astype(BF16)
    mus = [mu[o[i]:o[i + 1]] for i in range(4)]
    padv = lambda v, n: jnp.pad(v, (0, n - v.shape[0]))
    mu_p = jnp.concatenate([mus[0], padv(mus[1], LANES), padv(mus[2], LANES),
                            padv(mus[3], 2 * LANES)]).reshape(1, RW_Z_W)
    vec = lambda v: v.reshape(1, W)
    full = lambda a: pl.BlockSpec(a.shape, lambda b, s: (0,) * a.ndim)
    args = [norm1.reshape(1, D), w_row, mu_p, vec(w0), _pad_rows(w2, LANES).astype(BF16), vec(a0),
            _pad_rows(a2, LANES).astype(BF16), _pad_rows(g2, 2 * LANES).astype(BF16), vec(k_k), vec(k_a)]
    oblk = pl.BlockSpec((1, TM, W), lambda b, s: (b, s, 0))
    return pl.pallas_call(
        _rw_prep_kernel,
        grid=(B, S // TM),
        in_specs=[pl.BlockSpec((1, TM, D), lambda b, s: (b, s, 0))] + [full(a) for a in args],
        out_specs=[oblk] * 7 + [pl.BlockSpec((1, TM, MEM_WIDTH), lambda b, s: (b, s, 0))],
        out_shape=[jax.ShapeDtypeStruct((B, S, W), F32)] * 7
        + [jax.ShapeDtypeStruct((B, S, MEM_WIDTH), BF16)],
        scratch_shapes=[pltpu.VMEM((8, RW_Z_W), F32)],
        compiler_params=_cparams(("parallel", "arbitrary")),
        name="rw_prep",
    )(x, *args)


def _split2(x):
    hi = x.astype(BF16)
    return hi, (x - hi.astype(F32)).astype(BF16)


def _mm3(a, b):
    dot = functools.partial(jnp.dot, preferred_element_type=F32)
    return dot(a[0], b[0]) + dot(a[0], b[1]) + dot(a[1], b[0])


def _each(f, *lists):
    return [f(*args) for args in zip(*lists)]


def _tri_inverse(a2s, eye, same16, same32, same64):
    ds = _each(lambda a2: jnp.where(same16, a2, 0.0), a2s)
    xs = _each(lambda d: eye + d, ds)
    for _ in range(3):
        ds = _each(lambda d: _mm(d, d), ds)
        xs = _each(lambda x, d: x + _mm(x, d), xs, ds)
    for lo, hi_ in ((same16, same32), (same32, same64)):
        sel = hi_ & jnp.logical_not(lo)
        mids = _each(lambda x, a2: _mm(x, jnp.where(sel, a2, 0.0)), xs, a2s)
        xs = _each(lambda x, m: x + _mm(m, x), xs, mids)
    res = _each(lambda x, a2: (eye - x) + _mm3(_split2(a2), _split2(x)), xs, a2s)
    return _each(lambda x, r: x + _mm(x, r), xs, res)


def _rw_scan_kernel(r_ref, lw_ref, kx_ref, km_ref, v_ref, a_ref, g_ref, rk_ref, lnw_ref, lnb_ref,
                    o_ref, st_scr, *, TS, C):
    Dh = HEAD_DIM

    @pl.when(pl.program_id(1) == 0)
    def _():
        st_scr[...] = jnp.zeros(st_scr.shape, F32)

    head0 = _iota((1, LANES), 1) < Dh
    ltri = jnp.where(_iota((C, C), 1) <= _iota((C, C), 0), 1.0, 0.0)
    col2 = _iota((C, LANES), 1) & (Dh - 1)
    row2 = _iota((C, LANES), 0)
    m_incl = col2 <= row2
    m_strict = col2 < row2
    r128 = _iota((LANES, LANES), 0)
    c128 = _iota((LANES, LANES), 1)
    eye = jnp.where(r128 == c128, 1.0, 0.0)
    same16 = (r128 >> 4) == (c128 >> 4)
    same32 = (r128 >> 5) == (c128 >> 5)
    same64 = (r128 >> 6) == (c128 >> 6)
    zeros = jnp.zeros((C, LANES), F32)
    ltri = ltri.astype(BF16)

    def hsum(x):
        s0 = jnp.sum(jnp.where(head0, x, 0.0), axis=-1, keepdims=True)
        s1 = jnp.sum(jnp.where(head0, 0.0, x), axis=-1, keepdims=True)
        return jnp.where(head0, s0, s1)

    n_pairs = MIX_WIDTH // LANES
    N_AHEAD = 2
    lns = [slice(pi * LANES, (pi + 1) * LANES) for pi in range(n_pairs)]
    pis = list(range(n_pairs))

    def cumsum_decay(lw):
        l1 = lw.astype(BF16)
        l2 = (lw - l1.astype(F32)).astype(BF16)
        l3 = (lw - l1.astype(F32) - l2.astype(F32)).astype(BF16)
        cum3 = jnp.dot(ltri, jnp.concatenate([l1, l2, l3], axis=1), preferred_element_type=F32)
        return cum3[:, :LANES] + cum3[:, LANES:2 * LANES] + cum3[:, 2 * LANES:]

    def scaled(ln, cum, sl):
        r, lw, kx, km, a = r_ref[0, sl, ln], lw_ref[0, sl, ln], kx_ref[0, sl, ln], km_ref[0, sl, ln], a_ref[0, sl, ln]
        kk = kx / jnp.maximum(jnp.sqrt(hsum(kx * kx)), 1e-12)
        p_in = jnp.exp(cum)
        at = -kk * jnp.exp(cum - lw)
        rt = r * p_in
        p_inv = jnp.exp(-cum)
        bk = jnp.concatenate([kk * a * p_inv, km * p_inv], axis=0)
        at0, at1 = jnp.where(head0, at, 0.0), jnp.where(head0, 0.0, at)
        rt0, rt1 = jnp.where(head0, rt, 0.0), jnp.where(head0, 0.0, rt)
        lhs = jnp.concatenate([at0, at1, rt0, rt1], axis=0)
        return lhs, bk, rt, p_in[C - 1:C, :]

    def split_aa(aa):
        aa0 = jnp.where(m_strict, aa[0:C], 0.0)
        aa1 = pltpu.roll(jnp.where(m_strict, aa[C:2 * C], 0.0), Dh, 1)
        ar0 = jnp.where(m_incl, aa[2 * C:3 * C], 0.0)
        ar1 = jnp.where(m_incl, aa[3 * C:4 * C], 0.0)
        a2 = jnp.concatenate([jnp.where(head0, aa0, 0.0), jnp.where(head0, 0.0, aa1)], axis=0)
        return aa0, aa1, ar0, ar1, a2

    def epilogue(ln, y, sl):
        r, km, v = r_ref[0, sl, ln], km_ref[0, sl, ln], v_ref[0, sl, ln]
        mean = hsum(y) * (1.0 / Dh)
        d = y - mean
        var = hsum(d * d) * (1.0 / Dh)
        yn = d * lax.rsqrt(var + GN_EPS) * lnw_ref[:, ln] + lnb_ref[:, ln]
        bonus = hsum(r * km * rk_ref[:, ln]) * v
        o_ref[0, sl, ln] = ((yn + bonus) * g_ref[0, sl, ln]).astype(BF16)

    def chunk_group(cg, carry):
        sls = [pl.ds(pl.multiple_of((cg * N_AHEAD + j) * C, C), C) for j in range(N_AHEAD)]
        sl_i = [sl for sl in sls for _ in lns]
        ln_i = [ln for _ in sls for ln in lns]
        cums = _each(lambda sl, ln: cumsum_decay(lw_ref[0, sl, ln]), sl_i, ln_i)
        lhss, bks, rts, pcs = zip(*_each(scaled, ln_i, cums, sl_i))
        aas = _each(lambda lhs, bk: _mm(lhs, bk, _NT), lhss, bks)
        aa0s, aa1s, ar0s, ar1s, a2s = zip(*_each(split_aa, aas))
        t2s = _tri_inverse(a2s, eye, same16, same32, same64)
        vs = _each(lambda sl, ln: v_ref[0, sl, ln], sl_i, ln_i)
        x0s = _each(lambda aa0, v: _mm(aa0, jnp.concatenate([zeros, v], axis=0)), aa0s, vs)
        x1s = _each(lambda aa1, v: _mm(aa1, jnp.concatenate([v, zeros], axis=0)), aa1s, vs)
        wus = _each(lambda t2, lhs, x0, x1: _mm(t2, jnp.concatenate(
            [lhs[:2 * C], jnp.concatenate([x0, x1], axis=0)], axis=1)), t2s, lhss, x0s, x1s)
        for j, sl in enumerate(sls):
            k = slice(j * n_pairs, (j + 1) * n_pairs)
            sts = _each(lambda pi: st_scr[pi], pis)
            us = _each(lambda wu, st: _mm(wu[:C, :LANES] + wu[C:, :LANES], st, _NT)
                       + jnp.where(head0, wu[:C, LANES:], wu[C:, LANES:]), wus[k], sts)
            uvs = _each(lambda u, v: jnp.concatenate([u, v], axis=0), us, vs[k])
            ys = _each(lambda rt, st, ar0, ar1, uv: _mm(rt, st, _NT)
                       + jnp.where(head0, _mm(ar0, uv), _mm(ar1, uv)), rts[k], sts, ar0s[k], ar1s[k], uvs)
            new = _each(lambda st, pc, uv, bk: st * pc + jnp.where(same64, _mm(uv, bk * pc, _TN), 0.0),
                        sts, pcs[k], uvs, bks[k])
            for pi in pis:
                st_scr[pi] = new[pi]
            _each(lambda ln, y: epilogue(ln, y, sl), lns, ys)
        return carry

    lax.fori_loop(0, TS // (C * N_AHEAD), chunk_group, 0)


def _rw_scan(r, lw, kx, km, v, a, g, r_k, lnx_w, lnx_b):
    B, S, W = r.shape
    TS = min(256, S)
    C = 64
    blk = pl.BlockSpec((1, TS, W), lambda b, s: (b, s, 0))
    vblk = pl.BlockSpec((1, W), lambda b, s: (0, 0))
    kern = functools.partial(_rw_scan_kernel, TS=TS, C=C)
    return pl.pallas_call(
        kern,
        grid=(B, S // TS),
        in_specs=[blk] * 7 + [vblk] * 3,
        out_specs=blk,
        out_shape=jax.ShapeDtypeStruct((B, S, W), BF16),
        scratch_shapes=[pltpu.VMEM((W // LANES, LANES, LANES), F32)],
        compiler_params=_cparams(("parallel", "arbitrary")),
        name="rw_scan",
    )(r, lw, kx, km, v, a, g, r_k.reshape(1, W), lnx_w.reshape(1, W), lnx_b.reshape(1, W))


def kernel(x, mem, norm1, norm_mem, w_mem_kv, w_o, norm2, w_ffn_in, w_ffn_out, nsa_w_in, nsa_gate_b,
           nsa_cmp_pos, nsa_cmp_w1, nsa_cmp_w2, rw_w_in, rw_mu, rw_w0, rw_w2, rw_a0, rw_a2, rw_g2,
           rw_k_k, rw_k_a, rw_r_k, rw_lnx_w, rw_lnx_b, final_norm):
    depth = norm1.shape[0]
    B, S, _ = x.shape
    ktm, vm = _mem_kv(mem, norm_mem, w_mem_kv)
    for i in range(depth):
        j = i // 2
        if i % 2 == 0:
            q, kc, vc, vs, vw, kst, kwt, qm, gates = _nsa_proj(x, norm1[i], nsa_w_in[j], nsa_gate_b[j])
            kcb, vcb = _compress(kc, vc, nsa_cmp_pos[j], nsa_cmp_w1[j], nsa_cmp_w2[j])
            mix = _nsa_attn(q, kcb, vcb, kst, vs, kwt, vw, gates)
        else:
            r, lw, kx, km, v, a, g, qm = _rw_prep(x, norm1[i], rw_w_in[j], rw_mu[j], rw_w0[j], rw_w2[j],
                                                  rw_a0[j], rw_a2[j], rw_g2[j], rw_k_k[j], rw_k_a[j])
            mix = _rw_scan(r, lw, kx, km, v, a, g, rw_r_k[j], rw_lnx_w[j], rw_lnx_b[j])
        x = _post(x, mix, qm, ktm[i], vm[i], w_o[i], norm2[i], w_ffn_in[i], w_ffn_out[i], final_norm,
                  final=(i == depth - 1))
    return x
```

```python
import functools
import math

import numpy as np
import jax
import jax.numpy as jnp
from jax import lax
from jax.experimental import pallas as pl
from jax.experimental.pallas import tpu as pltpu

F32 = jnp.float32
BF16 = jnp.bfloat16
HI = lax.Precision.HIGHEST

D_MODEL = 1024
HEAD_DIM = 64
MIX_WIDTH = 768
MIX_HEADS = 12
MEM_HEADS = 4
MEM_WIDTH = 256
KV_GROUPS = 4
GROUP_SIZE = 3
KV_WIDTH = 256
CMP_BLOCK = 32
CMP_STRIDE = 16
CMP_HIDDEN = 128
SEL_BLOCK = 64
SEL_TOPN = 16
WINDOW = 512
FORCE_SCORE = 1.0e4
DECAY_LORA = 64
ICLR_LORA = 64
GATE_LORA = 160
GN_EPS = 64e-5
FFN_HIDDEN = 2816
RMS_EPS = 1e-6
NEG_INF = -1e30
LOG2E = 1.4426950408889634

LANES = 128
VMEM_LIMIT = 56 * 1024 * 1024

_NT = (((1,), (1,)), ((), ()))
_TN = (((0,), (0,)), ((), ()))


def _mm(a, b, dims=None, hi=False):
    if hi:
        a, b, prec = a.astype(F32), b.astype(F32), HI
    else:
        a, b, prec = a.astype(BF16), b.astype(BF16), None
    if dims is None:
        return jnp.dot(a, b, preferred_element_type=F32, precision=prec)
    return lax.dot_general(a, b, dims, preferred_element_type=F32, precision=prec)


def _iota(shape, dim):
    return lax.broadcasted_iota(jnp.int32, shape, dim)


def _rms(x, g):
    ms = jnp.mean(x * x, axis=-1, keepdims=True)
    return x * lax.rsqrt(ms + RMS_EPS) * g


def _cparams(sem):
    return pltpu.CompilerParams(dimension_semantics=sem, vmem_limit_bytes=VMEM_LIMIT)


def _each(f, *lists):
    return [f(*args) for args in zip(*lists)]


def _mem_kv_kernel(mem_ref, g_ref, wkt_ref, wv_ref, kt_ref, v_ref):
    mn = _rms(mem_ref[0], g_ref[0]).astype(BF16)
    kt = _mm(wkt_ref[0], mn, _NT)
    v = _mm(mn, wv_ref[0])
    rowh = _iota(kt.shape, 0) // HEAD_DIM
    colh = _iota(v.shape, 1) // HEAD_DIM
    for h in range(MEM_HEADS):
        kt_ref[0, 0, h] = jnp.where(rowh == h, kt, 0.0).astype(BF16)
        v_ref[0, 0, h] = jnp.where(colh == h, v, 0.0).astype(BF16)


def _mem_kv(mem, norm_mem, w_mem_kv):
    B, M, D = mem.shape
    L = norm_mem.shape[0]
    wkt = jnp.swapaxes(w_mem_kv[:, :, :MEM_WIDTH], 1, 2).astype(BF16)
    wv = w_mem_kv[:, :, MEM_WIDTH:].astype(BF16)
    return pl.pallas_call(
        _mem_kv_kernel,
        grid=(L, B),
        in_specs=[
            pl.BlockSpec((1, M, D), lambda l, b: (b, 0, 0)),
            pl.BlockSpec((1, 1, D), lambda l, b: (l, 0, 0)),
            pl.BlockSpec((1, MEM_WIDTH, D), lambda l, b: (l, 0, 0)),
            pl.BlockSpec((1, D, MEM_WIDTH), lambda l, b: (l, 0, 0)),
        ],
        out_specs=[
            pl.BlockSpec((1, 1, MEM_HEADS, MEM_WIDTH, M), lambda l, b: (l, b, 0, 0, 0)),
            pl.BlockSpec((1, 1, MEM_HEADS, M, MEM_WIDTH), lambda l, b: (l, b, 0, 0, 0)),
        ],
        out_shape=[
            jax.ShapeDtypeStruct((L, B, MEM_HEADS, MEM_WIDTH, M), BF16),
            jax.ShapeDtypeStruct((L, B, MEM_HEADS, M, MEM_WIDTH), BF16),
        ],
        compiler_params=_cparams(("parallel", "parallel")),
        name="mem_kv",
    )(mem, norm_mem.reshape(L, 1, D), wkt, wv)


POST_CHUNK = 256


def _post_kernel(x_ref, mix_ref, qm_ref, kt_ref, v_ref, wo_ref, g2_ref, wg_ref, wu_ref, wout_ref,
                 gf_ref, o_ref, x1_scr, hn_scr, acc_scr, *, n_h, final):
    h = pl.program_id(2)

    @pl.when(h == 0)
    def _():
        tm = qm_ref.shape[1]
        x1 = x_ref[0] + _mm(mix_ref[0], wo_ref[:MIX_WIDTH])
        qs = [qm_ref[0, c * POST_CHUNK:(c + 1) * POST_CHUNK, :] for c in range(tm // POST_CHUNK)]
        cross = [jnp.zeros(q.shape, F32) for q in qs]
        for hd in range(MEM_HEADS):
            s = _each(lambda q: _mm(q, kt_ref[0, hd]), qs)
            p = _each(lambda s_: jnp.exp(s_ - jnp.max(s_, axis=-1, keepdims=True)), s)
            pv = _each(lambda p_: _mm(p_ * (1.0 / jnp.sum(p_, axis=-1, keepdims=True)), v_ref[0, hd]), p)
            cross = _each(lambda a, b: a + b, cross, pv)
        x1 = x1 + _mm(jnp.concatenate(cross, axis=0), wo_ref[MIX_WIDTH:])
        x1_scr[...] = x1
        hn_scr[...] = _rms(x1, g2_ref[...]).astype(BF16)
        acc_scr[...] = jnp.zeros(acc_scr.shape, F32)

    hn = hn_scr[...]
    gate = _mm(hn, wg_ref[...])
    up = _mm(hn, wu_ref[...])
    hid = gate * jax.nn.sigmoid(gate) * up
    acc_scr[...] += _mm(hid, wout_ref[...])

    @pl.when(h == n_h - 1)
    def _():
        y = x1_scr[...] + acc_scr[...]
        if final:
            y = _rms(y, gf_ref[...])
        o_ref[0] = y


def _post(x, mix, qm, ktm, vm, w_o, norm2, w_ffn_in, w_ffn_out, final_norm, final):
    B, S, D = x.shape
    M = ktm.shape[-1]
    TM = min(1024, S)
    TH = 256
    NH = FFN_HIDDEN // TH
    wi = w_ffn_in.astype(BF16)
    kern = functools.partial(_post_kernel, n_h=NH, final=final)
    return pl.pallas_call(
        kern,
        grid=(B, S // TM, NH),
        in_specs=[
            pl.BlockSpec((1, TM, D), lambda b, s, h: (b, s, 0)),
            pl.BlockSpec((1, TM, MIX_WIDTH), lambda b, s, h: (b, s, 0)),
            pl.BlockSpec((1, TM, MEM_WIDTH), lambda b, s, h: (b, s, 0)),
            pl.BlockSpec((1, MEM_HEADS, MEM_WIDTH, M), lambda b, s, h: (b, 0, 0, 0)),
            pl.BlockSpec((1, MEM_HEADS, M, MEM_WIDTH), lambda b, s, h: (b, 0, 0, 0)),
            pl.BlockSpec((MIX_WIDTH + MEM_WIDTH, D), lambda b, s, h: (0, 0)),
            pl.BlockSpec((1, D), lambda b, s, h: (0, 0)),
            pl.BlockSpec((D, TH), lambda b, s, h: (0, h)),
            pl.BlockSpec((D, TH), lambda b, s, h: (0, NH + h)),
            pl.BlockSpec((TH, D), lambda b, s, h: (h, 0)),
            pl.BlockSpec((1, D), lambda b, s, h: (0, 0)),
        ],
        out_specs=pl.BlockSpec((1, TM, D), lambda b, s, h: (b, s, 0)),
        out_shape=jax.ShapeDtypeStruct((B, S, D), F32),
        scratch_shapes=[
            pltpu.VMEM((TM, D), F32),
            pltpu.VMEM((TM, D), BF16),
            pltpu.VMEM((TM, D), F32),
        ],
        compiler_params=_cparams(("parallel", "parallel", "arbitrary")),
        name="post_ffn",
    )(x, mix, qm, ktm, vm, w_o.astype(BF16), norm2.reshape(1, D), wi, wi, w_ffn_out.astype(BF16),
      final_norm.reshape(1, D))


NSA_ROW_W = MIX_WIDTH + 4 * KV_WIDTH + MEM_WIDTH + LANES


def _nsa_proj_kernel(x_ref, g_ref, w_ref, wt_ref, gb_ref, qc_ref, q_ref, kc_ref, vc_ref, vs_ref, vw_ref,
                     kst_ref, kwt_ref, qm_ref, gate_ref, cmp_scr):
    hn = _rms(x_ref[0], g_ref[...]).astype(BF16)
    res = _mm(hn, w_ref[...])
    tm = res.shape[0]
    for h in range(MIX_HEADS):
        q_ref[0, h] = jnp.concatenate(
            [res[:, h * HEAD_DIM:(h + 1) * HEAD_DIM], jnp.broadcast_to(qc_ref[h], (tm, HEAD_DIM))],
            axis=1).astype(BF16)
    ones_col = jnp.ones((tm, HEAD_DIM), F32)
    off = MIX_WIDTH
    for ref in (kc_ref, vc_ref):
        for half in range(KV_WIDTH // LANES):
            cmp_scr[...] = res[:, off:off + LANES]
            toks = [cmp_scr[pl.ds(l, tm // CMP_STRIDE, stride=CMP_STRIDE), :] for l in range(CMP_STRIDE)]
            for gg in range(LANES // HEAD_DIM):
                ref[0, 2 * half + gg] = jnp.concatenate(
                    [t[:, gg * HEAD_DIM:(gg + 1) * HEAD_DIM] for t in toks], axis=1).astype(BF16)
            off += LANES
    for ref in (vs_ref, vw_ref):
        for g in range(KV_GROUPS):
            t = res[:, off + g * HEAD_DIM: off + (g + 1) * HEAD_DIM]
            ref[0, g] = jnp.concatenate([t, ones_col], axis=1).astype(BF16)
        off += KV_WIDTH
    qm_ref[0] = res[:, off:off + MEM_WIDTH].astype(BF16)
    off += MEM_WIDTH
    gates = jax.nn.sigmoid(res[:, off:off + 64] + gb_ref[...])
    for g in range(KV_GROUPS):
        gate_ref[0, g] = gates[:, g * 16:(g + 1) * 16]
    rt = _mm(wt_ref[...], hn, _NT)
    kst_ref[0] = rt[:KV_WIDTH].astype(BF16)
    kwt_ref[0] = rt[KV_WIDTH:].astype(BF16)


def _alibi_slopes(n):
    def pow2(m):
        start = 2.0 ** (-8.0 / m)
        return [start ** (i + 1) for i in range(m)]
    c = 2 ** int(math.floor(math.log2(n)))
    s = pow2(c)
    if c < n:
        s = s + pow2(2 * c)[0::2][: n - c]
    return np.asarray(s, dtype=np.float32)


def _pos_pieces(pos):
    a64 = (pos >> 6) * 64
    b = pos & 63
    return np.stack([a64, a64, a64, b, b, b]).astype(np.float32)


def _slope_pieces():
    sl = jnp.asarray(_alibi_slopes(MIX_HEADS) * np.float32(LOG2E), F32)
    s1 = sl.astype(BF16).astype(F32)
    s2 = (sl - s1).astype(BF16).astype(F32)
    s3 = (sl - s1 - s2).astype(BF16).astype(F32)
    six = jnp.stack([s1, s2, s3, s1, s2, s3], axis=1)
    return jnp.pad(six, ((0, 0), (0, HEAD_DIM - 6))).reshape(MIX_HEADS, 1, HEAD_DIM)


def _nsa_proj(x, norm1, w_in, gate_b):
    B, S, D = x.shape
    TM = min(512, S)
    G, H, Dh = KV_GROUPS, MIX_HEADS, HEAD_DIM
    scale = HEAD_DIM ** -0.5
    o = np.cumsum([0, MIX_WIDTH] + [KV_WIDTH] * 6 + [3 * MIX_HEADS, MEM_WIDTH])
    wq, wkc, wvc, wks, wvs, wkw, wvw, wgl, wqm = (w_in[:, o[i]:o[i + 1]] for i in range(9))
    wgl = wgl.reshape(D, 3, G, GROUP_SIZE).transpose(0, 2, 1, 3).reshape(D, G, 9)
    wgl = jnp.pad(wgl, ((0, 0), (0, 0), (0, 7))).reshape(D, 64)
    gb = gate_b.reshape(3, G, GROUP_SIZE).transpose(1, 0, 2).reshape(G, 9)
    gb = jnp.pad(gb, ((0, 0), (0, 7))).reshape(1, 64)
    w_row = jnp.concatenate(
        [wq * (scale * LOG2E), wkc, wvc, wvs, wvw, wqm * scale, wgl, jnp.zeros((D, LANES - 64), F32)],
        axis=1).astype(BF16)
    w_t = jnp.concatenate([wks, wkw], axis=1).T.astype(BF16)
    head = lambda n, w: pl.BlockSpec((1, n, TM, w), lambda b, s: (b, 0, s, 0))
    cmpv = pl.BlockSpec((1, G, TM // CMP_STRIDE, CMP_STRIDE * Dh), lambda b, s: (b, 0, s, 0))
    return pl.pallas_call(
        _nsa_proj_kernel,
        grid=(B, S // TM),
        in_specs=[
            pl.BlockSpec((1, TM, D), lambda b, s: (b, s, 0)),
            pl.BlockSpec((1, D), lambda b, s: (0, 0)),
            pl.BlockSpec((D, NSA_ROW_W), lambda b, s: (0, 0)),
            pl.BlockSpec((2 * KV_WIDTH, D), lambda b, s: (0, 0)),
            pl.BlockSpec((1, 64), lambda b, s: (0, 0)),
            pl.BlockSpec((H, 1, Dh), lambda b, s: (0, 0, 0)),
        ],
        out_specs=[
            head(H, LANES), cmpv, cmpv, head(G, LANES), head(G, LANES),
            pl.BlockSpec((1, KV_WIDTH, TM), lambda b, s: (b, 0, s)),
            pl.BlockSpec((1, KV_WIDTH, TM), lambda b, s: (b, 0, s)),
            pl.BlockSpec((1, TM, MEM_WIDTH), lambda b, s: (b, s, 0)),
            pl.BlockSpec((1, G, TM, 16), lambda b, s: (b, 0, s, 0)),
        ],
        out_shape=[
            jax.ShapeDtypeStruct((B, H, S, LANES), BF16),
            jax.ShapeDtypeStruct((B, G, S // CMP_STRIDE, CMP_STRIDE * Dh), BF16),
            jax.ShapeDtypeStruct((B, G, S // CMP_STRIDE, CMP_STRIDE * Dh), BF16),
            jax.ShapeDtypeStruct((B, G, S, LANES), BF16),
            jax.ShapeDtypeStruct((B, G, S, LANES), BF16),
            jax.ShapeDtypeStruct((B, KV_WIDTH, S), BF16),
            jax.ShapeDtypeStruct((B, KV_WIDTH, S), BF16),
            jax.ShapeDtypeStruct((B, S, MEM_WIDTH), BF16),
            jax.ShapeDtypeStruct((B, G, S, 16), F32),
        ],
        scratch_shapes=[pltpu.VMEM((TM, LANES), F32)],
        compiler_params=_cparams(("parallel", "parallel")),
        name="nsa_proj",
    )(x, norm1.reshape(1, D), w_row, w_t, gb, _slope_pieces())


def _compress_kernel(kc_ref, vc_ref, w1_ref, w2_ref, pos_ref, cpos_ref, kcb_ref, vcb_ref):
    half = CMP_STRIDE * HEAD_DIM
    for idx, (src, dst) in enumerate(((kc_ref, kcb_ref), (vc_ref, vcb_ref))):
        c = src[0, 0]
        ncp = c.shape[0]
        a = _mm(c, w1_ref[idx, :half])
        bm = _mm(c, w1_ref[idx, half:])
        bias = _mm(pos_ref[idx], w1_ref[idx])[0:1]
        hid = jax.nn.gelu(a + pltpu.roll(bm, ncp - 1, 0) + bias)
        out = _mm(hid, w2_ref[idx])
        out = jnp.where(_iota(out.shape, 0) < ncp - 1, out, 0.0)
        pad = cpos_ref[...] if idx == 0 else jnp.zeros(out.shape, F32)
        out = jnp.concatenate([out, pad], axis=1)
        dst[0, 0] = out.astype(BF16)


def _compress(kc, vc, cmp_pos, cmp_w1, cmp_w2):
    B, G, NCP, _ = kc.shape
    Dh = HEAD_DIM
    w1 = cmp_w1.reshape(2, CMP_BLOCK * Dh, CMP_HIDDEN).astype(BF16)
    w2 = cmp_w2.astype(BF16)
    pos = jnp.broadcast_to(cmp_pos.reshape(2, 1, CMP_BLOCK * Dh), (2, 8, CMP_BLOCK * Dh)).astype(BF16)
    cend = np.arange(NCP) * CMP_STRIDE + CMP_BLOCK - 1
    cpos = np.zeros((NCP, Dh), np.float32)
    cpos[:, :6] = _pos_pieces(cend).T
    blk = pl.BlockSpec((1, 1, NCP, CMP_STRIDE * Dh), lambda b, g: (b, g, 0, 0))
    return pl.pallas_call(
        _compress_kernel,
        grid=(B, G),
        in_specs=[
            blk, blk,
            pl.BlockSpec((2, CMP_BLOCK * Dh, CMP_HIDDEN), lambda b, g: (0, 0, 0)),
            pl.BlockSpec((2, CMP_HIDDEN, Dh), lambda b, g: (0, 0, 0)),
            pl.BlockSpec((2, 8, CMP_BLOCK * Dh), lambda b, g: (0, 0, 0)),
            pl.BlockSpec((NCP, Dh), lambda b, g: (0, 0)),
        ],
        out_specs=[pl.BlockSpec((1, 1, NCP, LANES), lambda b, g: (b, g, 0, 0))] * 2,
        out_shape=[jax.ShapeDtypeStruct((B, G, NCP, LANES), BF16)] * 2,
        compiler_params=_cparams(("parallel", "parallel")),
        name="nsa_compress",
    )(kc, vc, w1, w2, pos, jnp.asarray(cpos))


def _exp2_bf16(s, m):
    return jnp.concatenate(
        [jnp.exp2((s[:, i * LANES:(i + 1) * LANES] - m).astype(BF16)) for i in range(s.shape[1] // LANES)],
        axis=1)


def _nsa_attn_kernel(q_ref, kcb_ref, vcb_ref, ovt_ref, kst_ref, ksc_ref, vs_ref, kwt_ref, kwc_ref,
                     vw_ref, gate_ref, eg_ref, o_ref, score_scr, m_scr, acc_scr, s_scr, sw_scr, *, TQ, TK, S, n_sel, NG):
    R, Dh, CH = GROUP_SIZE, HEAD_DIM, LANES
    t0 = pl.program_id(2) * TQ
    NCP, NB = S // CMP_STRIDE, S // SEL_BLOCK
    n_sub = TQ // CH
    n_ch = R * n_sub
    groups = list(range(NG))
    chunks = [(g, c) for g in groups for c in range(n_ch)]
    rows_of = lambda c: slice(c * CH, (c + 1) * CH)
    qa = [q_ref[0, g * R:(g + 1) * R].reshape(R * TQ, LANES) for g in groups]
    tq_col = t0 + _iota((TQ, 1), 0)
    tq_col3 = t0 + (_iota((R * TQ, 1), 0) & (TQ - 1))

    sc = [_mm(qa[g], kcb_ref[0, g], _NT) for g in groups]
    WK = WINDOW + TQ
    w0 = pl.multiple_of(jnp.maximum(t0 - WINDOW, 0), LANES)
    consts_w = kwc_ref[:, pl.ds(w0, WK)]
    for g in groups:
        sw_scr[g] = jnp.dot(qa[g], jnp.concatenate(
            [kwt_ref[0, g * Dh:(g + 1) * Dh, pl.ds(w0, WK)], consts_w], axis=0), preferred_element_type=F32)
    n_row = _iota((1, NCP), 1)
    cend = n_row * CMP_STRIDE + (CMP_BLOCK - 1)
    mask_c = (cend <= tq_col) & (n_row < NCP - 1)
    psum = [[jnp.zeros((CH, NCP), F32) for _ in range(n_sub)] for _ in groups]
    o_c = []
    for g, c in chunks:
        mk = mask_c[(c % n_sub) * CH:(c % n_sub + 1) * CH]
        s = jnp.where(mk, sc[g][rows_of(c)], NEG_INF)
        m = jnp.max(s, axis=-1, keepdims=True)
        p = jnp.where(mk, jnp.exp2(s - m), 0.0)
        l = jnp.sum(p, axis=-1, keepdims=True)
        p = p * jnp.where(l > 0.0, 1.0 / l, 0.0)
        psum[g][c % n_sub] = psum[g][c % n_sub] + p
        o_c.append(_mm(p, vcb_ref[0, g]))
    psum = [jnp.concatenate(psum[g], axis=0) for g in groups]
    o_c = [jnp.concatenate(o_c[g * n_ch:(g + 1) * n_ch], axis=0) for g in groups]

    tb = lax.shift_right_logical(t0 + _iota((1, TQ), 1), 6)
    j_col = _iota((NB, 1), 0)
    valid = j_col <= tb
    forced = (j_col == 0) | (j_col == tb) | (j_col == tb - 1)
    score = []
    for g in groups:
        imp_t = _mm(ovt_ref[...], psum[g], _NT, hi=True)
        score.append(jnp.where(valid, jnp.where(forced, FORCE_SCORE, imp_t), -jnp.inf))
        score_scr[g] = score[g]
    n_valid = (t0 + TQ - 1) // SEL_BLOCK + 1

    def rank_body(i, cnts):
        tie = jnp.where(i < j_col, 1.0, 0.0)
        out = []
        for g in groups:
            row = score_scr[g, pl.ds(i, 1), :]
            out.append(cnts[g] + jnp.where(row > score[g], 1.0, jnp.where(row == score[g], tie, 0.0)))
        return tuple(out)

    rank = lax.fori_loop(0, n_valid, rank_body, tuple(jnp.zeros((NB, TQ), F32) for _ in groups))
    q2 = []
    for g in groups:
        unsel_t = jnp.where(valid, jnp.where(rank[g] < n_sel, 0.0, 1.0), 1.0)
        unsel = jnp.concatenate([unsel_t, jnp.zeros((LANES - NB, TQ), F32)], axis=0).T.astype(BF16)
        q2.append(jnp.concatenate([qa[g], jnp.concatenate([unsel] * R, axis=0)], axis=1))

    def scores_to(slot, kt):
        k0 = pl.multiple_of(kt * TK, TK)
        consts = ksc_ref[:, pl.ds(k0, TK)]
        for g in groups:
            s_scr[slot, g] = jnp.dot(q2[g], jnp.concatenate(
                [kst_ref[0, g * Dh:(g + 1) * Dh, pl.ds(k0, TK)], consts], axis=0), preferred_element_type=F32)

    scores_to(0, 0)

    kpos_w = w0 + _iota((1, WK), 1)
    mask_w = []
    for h in range(n_sub):
        dist = (t0 + h * CH + _iota((CH, 1), 0)) - kpos_w
        mask_w.append((dist >= 0) & (dist < WINDOW))
    ow = []
    for g, c in chunks:
        s = jnp.where(mask_w[c % n_sub], sw_scr[g, rows_of(c), :], NEG_INF)
        m = jnp.broadcast_to(jnp.max(s, axis=-1, keepdims=True), (CH, LANES))
        ow.append(jnp.dot(_exp2_bf16(s, m), vw_ref[0, g, pl.ds(w0, WK), :], preferred_element_type=F32))
    ow = [jnp.concatenate(ow[g * n_ch:(g + 1) * n_ch], axis=0) for g in groups]

    m_scr[...] = jnp.full(m_scr.shape, NEG_INF, F32)
    acc_scr[...] = jnp.zeros(acc_scr.shape, F32)
    n_kt = (t0 + TQ - 1) // TK + 1

    def accumulate(kt, slot, causal):
        k0 = pl.multiple_of(kt * TK, TK)

        def tile(g, c):
            x = s_scr[slot, g, rows_of(c), :]
            if causal:
                keep = k0 + _iota((1, TK), 1) <= t0 + (c % n_sub) * CH + _iota((CH, 1), 0)
                x = jnp.where(keep, x, NEG_INF)
            return x

        m_old = [m_scr[g, rows_of(c)] for g, c in chunks]
        m_new = [jnp.maximum(mo, jnp.max(tile(g, c), axis=-1, keepdims=True))
                 for mo, (g, c) in zip(m_old, chunks)]
        for i, (g, c) in enumerate(chunks):
            p = _exp2_bf16(tile(g, c), m_new[i])
            pv = jnp.dot(p, vs_ref[0, g, pl.ds(k0, TK), :], preferred_element_type=F32)
            acc_scr[g, rows_of(c)] = jnp.exp2(m_old[i] - m_new[i]) * acc_scr[g, rows_of(c)] + pv
            m_scr[g, rows_of(c)] = m_new[i]

    n_full = n_kt - 1

    def pair_body(j, carry):
        scores_to(1, 2 * j + 1)
        accumulate(2 * j, 0, False)
        scores_to(0, 2 * j + 2)
        accumulate(2 * j + 1, 1, False)
        return carry

    lax.fori_loop(0, n_full // 2, pair_body, 0)

    @pl.when(n_full % 2 == 1)
    def _():
        scores_to(1, n_full)
        accumulate(n_full - 1, 0, False)
        accumulate(n_full, 1, True)

    @pl.when(n_full % 2 == 0)
    def _():
        accumulate(n_full, 0, True)

    for g in groups:
        acc = acc_scr[g]
        o_s = acc * (1.0 / pltpu.roll(acc, Dh, 1))
        o_w = ow[g] * (1.0 / pltpu.roll(ow[g], Dh, 1))
        gv = gate_ref[0, g]
        g_hi = gv.astype(BF16)
        g_lo = (gv - g_hi.astype(F32)).astype(BF16)
        ge = (jnp.dot(g_hi, eg_ref[...], preferred_element_type=F32)
              + jnp.dot(g_lo, eg_ref[...], preferred_element_type=F32))
        gate = lambda k: ge[:, k * LANES:(k + 1) * LANES]
        for r in range(R):
            rows = slice(r * TQ, (r + 1) * TQ)
            out = gate(r) * o_c[g][rows] + gate(R + r) * o_s[rows] + gate(2 * R + r) * o_w[rows]
            o_ref[0, :, (g * R + r) * Dh:(g * R + r + 1) * Dh] = out[:, :Dh].astype(BF16)


def _nsa_attn(q, kcb, vcb, kst, vs, kwt, vw, gates):
    B, H, S, _ = q.shape
    G, R, Dh = KV_GROUPS, GROUP_SIZE, HEAD_DIM
    TQ = 256
    TK = min(512, S)
    NG = 2
    NCP, NB = S // CMP_STRIDE, S // SEL_BLOCK
    n_sel = min(SEL_TOPN, NB)
    assert S % TK == 0 and S >= WINDOW + TQ and NB <= Dh and S <= 4096
    cs = np.arange(NCP) * CMP_STRIDE
    ss = np.arange(NB) * SEL_BLOCK
    ov = (cs[:, None] <= ss[None, :] + SEL_BLOCK - 1) & (cs[:, None] + CMP_BLOCK - 1 >= ss[None, :])
    ov[NCP - 1] = False
    ovt = jnp.asarray(ov.T.astype(np.float32))
    pieces = _pos_pieces(np.arange(S))
    ksc = np.zeros((3 * Dh, S), np.float32)
    ksc[:6] = pieces
    ksc[Dh:Dh + NB] = np.where(np.arange(S)[None, :] // SEL_BLOCK == np.arange(NB)[:, None], NEG_INF, 0.0)
    kwc = np.zeros((Dh, S), np.float32)
    kwc[:6] = pieces
    eg = (np.arange(9 * LANES)[None, :] // LANES == np.arange(16)[:, None]).astype(np.float32)
    kern = functools.partial(_nsa_attn_kernel, TQ=TQ, TK=TK, S=S, n_sel=n_sel, NG=NG)
    return pl.pallas_call(
        kern,
        grid=(B, G // NG, S // TQ),
        in_specs=[
            pl.BlockSpec((1, NG * R, TQ, LANES), lambda b, g, i: (b, g, i, 0)),
            pl.BlockSpec((1, NG, NCP, LANES), lambda b, g, i: (b, g, 0, 0)),
            pl.BlockSpec((1, NG, NCP, LANES), lambda b, g, i: (b, g, 0, 0)),
            pl.BlockSpec((NB, NCP), lambda b, g, i: (0, 0)),
            pl.BlockSpec((1, NG * Dh, S), lambda b, g, i: (b, g, 0)),
            pl.BlockSpec((3 * Dh, S), lambda b, g, i: (0, 0)),
            pl.BlockSpec((1, NG, S, LANES), lambda b, g, i: (b, g, 0, 0)),
            pl.BlockSpec((1, NG * Dh, S), lambda b, g, i: (b, g, 0)),
            pl.BlockSpec((Dh, S), lambda b, g, i: (0, 0)),
            pl.BlockSpec((1, NG, S, LANES), lambda b, g, i: (b, g, 0, 0)),
            pl.BlockSpec((1, NG, TQ, 16), lambda b, g, i: (b, g, i, 0)),
            pl.BlockSpec((16, 9 * LANES), lambda b, g, i: (0, 0)),
        ],
        out_specs=pl.BlockSpec((1, TQ, NG * R * Dh), lambda b, g, i: (b, i, g)),
        out_shape=jax.ShapeDtypeStruct((B, S, G * R * Dh), BF16),
        scratch_shapes=[
            pltpu.VMEM((NG, NB, TQ), F32),
            pltpu.VMEM((NG, R * TQ, LANES), F32),
            pltpu.VMEM((NG, R * TQ, LANES), F32),
            pltpu.VMEM((2, NG, R * TQ, TK), F32),
            pltpu.VMEM((NG, R * TQ, WINDOW + TQ), F32),
        ],
        compiler_params=_cparams(("parallel", "parallel", "arbitrary")),
        name="nsa_attn",
    )(q, kcb, vcb, ovt, kst, jnp.asarray(ksc, BF16), vs, kwt, jnp.asarray(kwc, BF16), vw, gates,
      jnp.asarray(eg, BF16))


RW_Z_W = 3 * MIX_WIDTH + 2 * LANES + 2 * LANES
RW_ROW_W = RW_Z_W + MEM_WIDTH


def _rw_prep_kernel(x_ref, g_ref, w_ref, mu_ref, w0_ref, w2_ref, a0_ref, a2_ref, g2_ref, kk_ref,
                    ka_ref, r_ref, lw_ref, kx_ref, km_ref, v_ref, a_ref, go_ref, qm_ref, carry_scr):
    W = MIX_WIDTH

    @pl.when(pl.program_id(1) == 0)
    def _():
        carry_scr[...] = jnp.zeros(carry_scr.shape, F32)

    hn = _rms(x_ref[0], g_ref[...]).astype(BF16)
    res = _mm(hn, w_ref[...])
    qm_ref[0] = res[:, RW_Z_W:].astype(BF16)
    z = res[:, :RW_Z_W]
    tm = z.shape[0]
    zprev = jnp.where(_iota((tm, 1), 0) == 0, carry_scr[0:1, :], pltpu.roll(z, 1, 0))
    carry_scr[0:1, :] = z[tm - 1:tm, :]
    z = z + (zprev - z) * mu_ref[...]
    r, k, v = z[:, :W], z[:, W:2 * W], z[:, 2 * W:3 * W]
    zw = z[:, 3 * W:3 * W + LANES]
    za = z[:, 3 * W + LANES:3 * W + 2 * LANES]
    zg = z[:, 3 * W + 2 * LANES:]
    w_log = -jax.nn.softplus(-(w0_ref[...] + _mm(jnp.tanh(zw), w2_ref[...]))) - 0.5
    a = jax.nn.sigmoid(a0_ref[...] + _mm(za, a2_ref[...]))
    r_ref[0] = r
    lw_ref[0] = -jnp.exp(w_log)
    kx_ref[0] = k * kk_ref[...]
    km_ref[0] = k * (1.0 + (a - 1.0) * ka_ref[...])
    v_ref[0] = v
    a_ref[0] = a
    go_ref[0] = _mm(jax.nn.sigmoid(zg), g2_ref[...])


def _pad_rows(w, n):
    return jnp.pad(w, ((0, n - w.shape[0]), (0, 0)))


def _rw_prep(x, norm1, w_in, mu, w0, w2, a0, a2, g2, k_k, k_a):
    B, S, D = x.shape
    W = MIX_WIDTH
    TM = min(256, S)
    o = np.cumsum([0, 3 * W, DECAY_LORA, ICLR_LORA, GATE_LORA, MEM_WIDTH])
    seg = [w_in[:, o[i]:o[i + 1]] for i in range(5)]
    padc = lambda w, n: jnp.pad(w, ((0, 0), (0, n - w.shape[1])))
    w_row = jnp.concatenate(
        [seg[0], padc(seg[1], LANES), padc(seg[2], LANES), padc(seg[3], 2 * LANES),
         seg[4] * HEAD_DIM ** -0.5], axis=1).astype(BF16)
    mus = [mu[o[i]:o[i + 1]] for i in range(4)]
    padv = lambda v, n: jnp.pad(v, (0, n - v.shape[0]))
    mu_p = jnp.concatenate([mus[0], padv(mus[1], LANES), padv(mus[2], LANES),
                            padv(mus[3], 2 * LANES)]).reshape(1, RW_Z_W)
    vec = lambda v: v.reshape(1, W)
    full = lambda a: pl.BlockSpec(a.shape, lambda b, s: (0,) * a.ndim)
    args = [norm1.reshape(1, D), w_row, mu_p, vec(w0), _pad_rows(w2, LANES).astype(BF16), vec(a0),
            _pad_rows(a2, LANES).astype(BF16), _pad_rows(g2, 2 * LANES).astype(BF16), vec(k_k), vec(k_a)]
    oblk = pl.BlockSpec((1, TM, W), lambda b, s: (b, s, 0))
    return pl.pallas_call(
        _rw_prep_kernel,
        grid=(B, S // TM),
        in_specs=[pl.BlockSpec((1, TM, D), lambda b, s: (b, s, 0))] + [full(a) for a in args],
        out_specs=[oblk] * 7 + [pl.BlockSpec((1, TM, MEM_WIDTH), lambda b, s: (b, s, 0))],
        out_shape=[jax.ShapeDtypeStruct((B, S, W), F32)] * 7
        + [jax.ShapeDtypeStruct((B, S, MEM_WIDTH), BF16)],
        scratch_shapes=[pltpu.VMEM((8, RW_Z_W), F32)],
        compiler_params=_cparams(("parallel", "arbitrary")),
        name="rw_prep",
    )(x, *args)


def _split2(x):
    hi = x.astype(BF16)
    return hi, (x - hi.astype(F32)).astype(BF16)


def _mm3(a, b):
    dot = functools.partial(jnp.dot, preferred_element_type=F32)
    return dot(a[0], b[0]) + dot(a[0], b[1]) + dot(a[1], b[0])


def _tri_inverse(a2s, eye, same16, same32, same64):
    ds = _each(lambda a2: jnp.where(same16, a2, 0.0), a2s)
    xs = _each(lambda d: eye + d, ds)
    for _ in range(3):
        ds = _each(lambda d: _mm(d, d), ds)
        xs = _each(lambda x, d: x + _mm(x, d), xs, ds)
    for lo, hi_ in ((same16, same32), (same32, same64)):
        sel = hi_ & jnp.logical_not(lo)
        mids = _each(lambda x, a2: _mm(x, jnp.where(sel, a2, 0.0)), xs, a2s)
        xs = _each(lambda x, m: x + _mm(m, x), xs, mids)
    res = _each(lambda x, a2: (eye - x) + _mm3(_split2(a2), _split2(x)), xs, a2s)
    return _each(lambda x, r: x + _mm(x, r), xs, res)


def _rw_scan_kernel(r_ref, lw_ref, kx_ref, km_ref, v_ref, a_ref, g_ref, rk_ref, lnw_ref, lnb_ref,
                    o_ref, st_scr, *, TS, C):
    Dh = HEAD_DIM

    @pl.when(pl.program_id(1) == 0)
    def _():
        st_scr[...] = jnp.zeros(st_scr.shape, F32)

    head0 = _iota((1, LANES), 1) < Dh
    ltri = jnp.where(_iota((C, C), 1) <= _iota((C, C), 0), 1.0, 0.0)
    col2 = _iota((C, LANES), 1) & (Dh - 1)
    row2 = _iota((C, LANES), 0)
    m_incl = col2 <= row2
    m_strict = col2 < row2
    r128 = _iota((LANES, LANES), 0)
    c128 = _iota((LANES, LANES), 1)
    eye = jnp.where(r128 == c128, 1.0, 0.0)
    same16 = (r128 >> 4) == (c128 >> 4)
    same32 = (r128 >> 5) == (c128 >> 5)
    same64 = (r128 >> 6) == (c128 >> 6)
    zeros = jnp.zeros((C, LANES), F32)
    ltri = ltri.astype(BF16)

    def hsum(x):
        s0 = jnp.sum(jnp.where(head0, x, 0.0), axis=-1, keepdims=True)
        s1 = jnp.sum(jnp.where(head0, 0.0, x), axis=-1, keepdims=True)
        return jnp.where(head0, s0, s1)

    n_pairs = MIX_WIDTH // LANES
    N_AHEAD = 2
    lns = [slice(pi * LANES, (pi + 1) * LANES) for pi in range(n_pairs)]
    pis = list(range(n_pairs))

    def cumsum_decay(lw):
        l1 = lw.astype(BF16)
        l2 = (lw - l1.astype(F32)).astype(BF16)
        l3 = (lw - l1.astype(F32) - l2.astype(F32)).astype(BF16)
        cum3 = jnp.dot(ltri, jnp.concatenate([l1, l2, l3], axis=1), preferred_element_type=F32)
        return cum3[:, :LANES] + cum3[:, LANES:2 * LANES] + cum3[:, 2 * LANES:]

    def scaled(ln, cum, sl):
        r, lw, kx, km, a = r_ref[0, sl, ln], lw_ref[0, sl, ln], kx_ref[0, sl, ln], km_ref[0, sl, ln], a_ref[0, sl, ln]
        kk = kx / jnp.maximum(jnp.sqrt(hsum(kx * kx)), 1e-12)
        p_in = jnp.exp(cum)
        at = -kk * jnp.exp(cum - lw)
        rt = r * p_in
        p_inv = jnp.exp(-cum)
        bk = jnp.concatenate([kk * a * p_inv, km * p_inv], axis=0)
        at0, at1 = jnp.where(head0, at, 0.0), jnp.where(head0, 0.0, at)
        rt0, rt1 = jnp.where(head0, rt, 0.0), jnp.where(head0, 0.0, rt)
        lhs = jnp.concatenate([at0, at1, rt0, rt1], axis=0)
        return lhs, bk, rt, p_in[C - 1:C, :]

    def split_aa(aa):
        aa0 = jnp.where(m_strict, aa[0:C], 0.0)
        aa1 = pltpu.roll(jnp.where(m_strict, aa[C:2 * C], 0.0), Dh, 1)
        ar0 = jnp.where(m_incl, aa[2 * C:3 * C], 0.0)
        ar1 = jnp.where(m_incl, aa[3 * C:4 * C], 0.0)
        a2 = jnp.concatenate([jnp.where(head0, aa0, 0.0), jnp.where(head0, 0.0, aa1)], axis=0)
        return aa0, aa1, ar0, ar1, a2

    def epilogue(ln, y, sl):
        r, km, v = r_ref[0, sl, ln], km_ref[0, sl, ln], v_ref[0, sl, ln]
        mean = hsum(y) * (1.0 / Dh)
        d = y - mean
        var = hsum(d * d) * (1.0 / Dh)
        yn = d * lax.rsqrt(var + GN_EPS) * lnw_ref[:, ln] + lnb_ref[:, ln]
        bonus = hsum(r * km * rk_ref[:, ln]) * v
        o_ref[0, sl, ln] = ((yn + bonus) * g_ref[0, sl, ln]).astype(BF16)

    def chunk_group(cg, carry):
        sls = [pl.ds(pl.multiple_of((cg * N_AHEAD + j) * C, C), C) for j in range(N_AHEAD)]
        sl_i = [sl for sl in sls for _ in lns]
        ln_i = [ln for _ in sls for ln in lns]
        cums = _each(lambda sl, ln: cumsum_decay(lw_ref[0, sl, ln]), sl_i, ln_i)
        lhss, bks, rts, pcs = zip(*_each(scaled, ln_i, cums, sl_i))
        aas = _each(lambda lhs, bk: _mm(lhs, bk, _NT), lhss, bks)
        aa0s, aa1s, ar0s, ar1s, a2s = zip(*_each(split_aa, aas))
        t2s = _tri_inverse(a2s, eye, same16, same32, same64)
        vs = _each(lambda sl, ln: v_ref[0, sl, ln], sl_i, ln_i)
        x0s = _each(lambda aa0, v: _mm(aa0, jnp.concatenate([zeros, v], axis=0)), aa0s, vs)
        x1s = _each(lambda aa1, v: _mm(aa1, jnp.concatenate([v, zeros], axis=0)), aa1s, vs)
        wus = _each(lambda t2, lhs, x0, x1: _mm(t2, jnp.concatenate(
            [lhs[:2 * C], jnp.concatenate([x0, x1], axis=0)], axis=1)), t2s, lhss, x0s, x1s)
        for j, sl in enumerate(sls):
            k = slice(j * n_pairs, (j + 1) * n_pairs)
            sts = _each(lambda pi: st_scr[pi], pis)
            us = _each(lambda wu, st: _mm(wu[:C, :LANES] + wu[C:, :LANES], st, _NT)
                       + jnp.where(head0, wu[:C, LANES:], wu[C:, LANES:]), wus[k], sts)
            uvs = _each(lambda u, v: jnp.concatenate([u, v], axis=0), us, vs[k])
            ys = _each(lambda rt, st, ar0, ar1, uv: _mm(rt, st, _NT)
                       + jnp.where(head0, _mm(ar0, uv), _mm(ar1, uv)), rts[k], sts, ar0s[k], ar1s[k], uvs)
            new = _each(lambda st, pc, uv, bk: st * pc + jnp.where(same64, _mm(uv, bk * pc, _TN), 0.0),
                        sts, pcs[k], uvs, bks[k])
            for pi in pis:
                st_scr[pi] = new[pi]
            _each(lambda ln, y: epilogue(ln, y, sl), lns, ys)
        return carry

    lax.fori_loop(0, TS // (C * N_AHEAD), chunk_group, 0)


def _rw_scan(r, lw, kx, km, v, a, g, r_k, lnx_w, lnx_b):
    B, S, W = r.shape
    TS = min(256, S)
    C = 64
    blk = pl.BlockSpec((1, TS, W), lambda b, s: (b, s, 0))
    vblk = pl.BlockSpec((1, W), lambda b, s: (0, 0))
    kern = functools.partial(_rw_scan_kernel, TS=TS, C=C)
    return pl.pallas_call(
        kern,
        grid=(B, S // TS),
        in_specs=[blk] * 7 + [vblk] * 3,
        out_specs=blk,
        out_shape=jax.ShapeDtypeStruct((B, S, W), BF16),
        scratch_shapes=[pltpu.VMEM((W // LANES, LANES, LANES), F32)],
        compiler_params=_cparams(("parallel", "arbitrary")),
        name="rw_scan",
    )(r, lw, kx, km, v, a, g, r_k.reshape(1, W), lnx_w.reshape(1, W), lnx_b.reshape(1, W))


def kernel(x, mem, norm1, norm_mem, w_mem_kv, w_o, norm2, w_ffn_in, w_ffn_out, nsa_w_in, nsa_gate_b,
           nsa_cmp_pos, nsa_cmp_w1, nsa_cmp_w2, rw_w_in, rw_mu, rw_w0, rw_w2, rw_a0, rw_a2, rw_g2,
           rw_k_k, rw_k_a, rw_r_k, rw_lnx_w, rw_lnx_b, final_norm):
    depth = norm1.shape[0]
    B, S, _ = x.shape
    ktm, vm = _mem_kv(mem, norm_mem, w_mem_kv)
    for i in range(depth):
        j = i // 2
        if i % 2 == 0:
            q, kc, vc, vs, vw, kst, kwt, qm, gates = _nsa_proj(x, norm1[i], nsa_w_in[j], nsa_gate_b[j])
            kcb, vcb = _compress(kc, vc, nsa_cmp_pos[j], nsa_cmp_w1[j], nsa_cmp_w2[j])
            mix = _nsa_attn(q, kcb, vcb, kst, vs, kwt, vw, gates)
        else:
            r, lw, kx, km, v, a, g, qm = _rw_prep(x, norm1[i], rw_w_in[j], rw_mu[j], rw_w0[j], rw_w2[j],
                                                  rw_a0[j], rw_a2[j], rw_g2[j], rw_k_k[j], rw_k_a[j])
            mix = _rw_scan(r, lw, kx, km, v, a, g, rw_r_k[j], rw_lnx_w[j], rw_lnx_b[j])
        x = _post(x, mix, qm, ktm[i], vm[i], w_o[i], norm2[i], w_ffn_in[i], w_ffn_out[i], final_norm,
                  final=(i == depth - 1))
    return x
```

```python
import functools
import math

import numpy as np
import jax
import jax.numpy as jnp
from jax import lax
from jax.experimental import pallas as pl
from jax.experimental.pallas import tpu as pltpu

F32 = jnp.float32
BF16 = jnp.bfloat16
HI = lax.Precision.HIGHEST

D_MODEL = 1024
HEAD_DIM = 64
MIX_WIDTH = 768
MIX_HEADS = 12
MEM_HEADS = 4
MEM_WIDTH = 256
KV_GROUPS = 4
GROUP_SIZE = 3
KV_WIDTH = 256
CMP_BLOCK = 32
CMP_STRIDE = 16
CMP_HIDDEN = 128
SEL_BLOCK = 64
SEL_TOPN = 16
WINDOW = 512
FORCE_SCORE = 1.0e4
DECAY_LORA = 64
ICLR_LORA = 64
GATE_LORA = 160
GN_EPS = 64e-5
FFN_HIDDEN = 2816
RMS_EPS = 1e-6
NEG_INF = -1e30
LOG2E = 1.4426950408889634

LANES = 128
VMEM_LIMIT = 56 * 1024 * 1024

_NT = (((1,), (1,)), ((), ()))
_TN = (((0,), (0,)), ((), ()))


def _mm(a, b, dims=None, hi=False):
    if hi:
        a, b, prec = a.astype(F32), b.astype(F32), HI
    else:
        a, b, prec = a.astype(BF16), b.astype(BF16), None
    if dims is None:
        return jnp.dot(a, b, preferred_element_type=F32, precision=prec)
    return lax.dot_general(a, b, dims, preferred_element_type=F32, precision=prec)


def _iota(shape, dim):
    return lax.broadcasted_iota(jnp.int32, shape, dim)


def _rms(x, g):
    ms = jnp.mean(x * x, axis=-1, keepdims=True)
    return x * lax.rsqrt(ms + RMS_EPS) * g


def _cparams(sem):
    return pltpu.CompilerParams(dimension_semantics=sem, vmem_limit_bytes=VMEM_LIMIT)


def _each(f, *lists):
    return [f(*args) for args in zip(*lists)]


def _mem_kv_kernel(mem_ref, g_ref, wkt_ref, wv_ref, kt_ref, v_ref):
    mn = _rms(mem_ref[0], g_ref[0]).astype(BF16)
    kt = _mm(wkt_ref[0], mn, _NT)
    v = _mm(mn, wv_ref[0])
    rowh = _iota(kt.shape, 0) // HEAD_DIM
    colh = _iota(v.shape, 1) // HEAD_DIM
    for h in range(MEM_HEADS):
        kt_ref[0, 0, h] = jnp.where(rowh == h, kt, 0.0).astype(BF16)
        v_ref[0, 0, h] = jnp.where(colh == h, v, 0.0).astype(BF16)


def _mem_kv(mem, norm_mem, w_mem_kv):
    B, M, D = mem.shape
    L = norm_mem.shape[0]
    wkt = jnp.swapaxes(w_mem_kv[:, :, :MEM_WIDTH], 1, 2).astype(BF16)
    wv = w_mem_kv[:, :, MEM_WIDTH:].astype(BF16)
    return pl.pallas_call(
        _mem_kv_kernel,
        grid=(L, B),
        in_specs=[
            pl.BlockSpec((1, M, D), lambda l, b: (b, 0, 0)),
            pl.BlockSpec((1, 1, D), lambda l, b: (l, 0, 0)),
            pl.BlockSpec((1, MEM_WIDTH, D), lambda l, b: (l, 0, 0)),
            pl.BlockSpec((1, D, MEM_WIDTH), lambda l, b: (l, 0, 0)),
        ],
        out_specs=[
            pl.BlockSpec((1, 1, MEM_HEADS, MEM_WIDTH, M), lambda l, b: (l, b, 0, 0, 0)),
            pl.BlockSpec((1, 1, MEM_HEADS, M, MEM_WIDTH), lambda l, b: (l, b, 0, 0, 0)),
        ],
        out_shape=[
            jax.ShapeDtypeStruct((L, B, MEM_HEADS, MEM_WIDTH, M), BF16),
            jax.ShapeDtypeStruct((L, B, MEM_HEADS, M, MEM_WIDTH), BF16),
        ],
        compiler_params=_cparams(("parallel", "parallel")),
        name="mem_kv",
    )(mem, norm_mem.reshape(L, 1, D), wkt, wv)


POST_CHUNK = 256


def _post_kernel(x_ref, mix_ref, qm_ref, kt_ref, v_ref, wo_ref, g2_ref, wg_ref, wu_ref, wout_ref,
                 gf_ref, o_ref, x1_scr, hn_scr, acc_scr, *, n_h, final):
    h = pl.program_id(2)

    @pl.when(h == 0)
    def _():
        tm = qm_ref.shape[1]
        x1 = x_ref[0] + _mm(mix_ref[0], wo_ref[:MIX_WIDTH])
        qs = [qm_ref[0, c * POST_CHUNK:(c + 1) * POST_CHUNK, :] for c in range(tm // POST_CHUNK)]
        cross = [jnp.zeros(q.shape, F32) for q in qs]
        for hd in range(MEM_HEADS):
            s = _each(lambda q: _mm(q, kt_ref[0, hd]), qs)
            p = _each(lambda s_: jnp.exp(s_ - jnp.max(s_, axis=-1, keepdims=True)), s)
            pv = _each(lambda p_: _mm(p_ * (1.0 / jnp.sum(p_, axis=-1, keepdims=True)), v_ref[0, hd]), p)
            cross = _each(lambda a, b: a + b, cross, pv)
        x1 = x1 + _mm(jnp.concatenate(cross, axis=0), wo_ref[MIX_WIDTH:])
        x1_scr[...] = x1
        hn_scr[...] = _rms(x1, g2_ref[...]).astype(BF16)
        acc_scr[...] = jnp.zeros(acc_scr.shape, F32)

    hn = hn_scr[...]
    gate = _mm(hn, wg_ref[...])
    up = _mm(hn, wu_ref[...])
    hid = gate * jax.nn.sigmoid(gate) * up
    acc_scr[...] += _mm(hid, wout_ref[...])

    @pl.when(h == n_h - 1)
    def _():
        y = x1_scr[...] + acc_scr[...]
        if final:
            y = _rms(y, gf_ref[...])
        o_ref[0] = y


def _post(x, mix, qm, ktm, vm, w_o, norm2, w_ffn_in, w_ffn_out, final_norm, final):
    B, S, D = x.shape
    M = ktm.shape[-1]
    TM = min(1024, S)
    TH = 256
    NH = FFN_HIDDEN // TH
    wi = w_ffn_in.astype(BF16)
    kern = functools.partial(_post_kernel, n_h=NH, final=final)
    return pl.pallas_call(
        kern,
        grid=(B, S // TM, NH),
        in_specs=[
            pl.BlockSpec((1, TM, D), lambda b, s, h: (b, s, 0)),
            pl.BlockSpec((1, TM, MIX_WIDTH), lambda b, s, h: (b, s, 0)),
            pl.BlockSpec((1, TM, MEM_WIDTH), lambda b, s, h: (b, s, 0)),
            pl.BlockSpec((1, MEM_HEADS, MEM_WIDTH, M), lambda b, s, h: (b, 0, 0, 0)),
            pl.BlockSpec((1, MEM_HEADS, M, MEM_WIDTH), lambda b, s, h: (b, 0, 0, 0)),
            pl.BlockSpec((MIX_WIDTH + MEM_WIDTH, D), lambda b, s, h: (0, 0)),
            pl.BlockSpec((1, D), lambda b, s, h: (0, 0)),
            pl.BlockSpec((D, TH), lambda b, s, h: (0, h)),
            pl.BlockSpec((D, TH), lambda b, s, h: (0, NH + h)),
            pl.BlockSpec((TH, D), lambda b, s, h: (h, 0)),
            pl.BlockSpec((1, D), lambda b, s, h: (0, 0)),
        ],
        out_specs=pl.BlockSpec((1, TM, D), lambda b, s, h: (b, s, 0)),
        out_shape=jax.ShapeDtypeStruct((B, S, D), F32),
        scratch_shapes=[
            pltpu.VMEM((TM, D), F32),
            pltpu.VMEM((TM, D), BF16),
            pltpu.VMEM((TM, D), F32),
        ],
        compiler_params=_cparams(("parallel", "parallel", "arbitrary")),
        name="post_ffn",
    )(x, mix, qm, ktm, vm, w_o.astype(BF16), norm2.reshape(1, D), wi, wi, w_ffn_out.astype(BF16),
      final_norm.reshape(1, D))


NSA_ROW_W = MIX_WIDTH + 4 * KV_WIDTH + MEM_WIDTH + LANES


def _nsa_proj_kernel(x_ref, g_ref, w_ref, wt_ref, gb_ref, qc_ref, q_ref, kc_ref, vc_ref, vs_ref, vw_ref,
                     kst_ref, kwt_ref, qm_ref, gate_ref, cmp_scr):
    hn = _rms(x_ref[0], g_ref[...]).astype(BF16)
    res = _mm(hn, w_ref[...])
    tm = res.shape[0]
    for h in range(MIX_HEADS):
        q_ref[0, h] = jnp.concatenate(
            [res[:, h * HEAD_DIM:(h + 1) * HEAD_DIM], jnp.broadcast_to(qc_ref[h], (tm, HEAD_DIM))],
            axis=1).astype(BF16)
    ones_col = jnp.ones((tm, HEAD_DIM), F32)
    off = MIX_WIDTH
    for ref in (kc_ref, vc_ref):
        for half in range(KV_WIDTH // LANES):
            cmp_scr[...] = res[:, off:off + LANES]
            toks = [cmp_scr[pl.ds(l, tm // CMP_STRIDE, stride=CMP_STRIDE), :] for l in range(CMP_STRIDE)]
            for gg in range(LANES // HEAD_DIM):
                ref[0, 2 * half + gg] = jnp.concatenate(
                    [t[:, gg * HEAD_DIM:(gg + 1) * HEAD_DIM] for t in toks], axis=1).astype(BF16)
            off += LANES
    for ref in (vs_ref, vw_ref):
        for g in range(KV_GROUPS):
            t = res[:, off + g * HEAD_DIM: off + (g + 1) * HEAD_DIM]
            ref[0, g] = jnp.concatenate([t, ones_col], axis=1).astype(BF16)
        off += KV_WIDTH
    qm_ref[0] = res[:, off:off + MEM_WIDTH].astype(BF16)
    off += MEM_WIDTH
    gates = jax.nn.sigmoid(res[:, off:off + 64] + gb_ref[...])
    for g in range(KV_GROUPS):
        gate_ref[0, g] = gates[:, g * 16:(g + 1) * 16]
    rt = _mm(wt_ref[...], hn, _NT)
    kst_ref[0] = rt[:KV_WIDTH].astype(BF16)
    kwt_ref[0] = rt[KV_WIDTH:].astype(BF16)


def _alibi_slopes(n):
    def pow2(m):
        start = 2.0 ** (-8.0 / m)
        return [start ** (i + 1) for i in range(m)]
    c = 2 ** int(math.floor(math.log2(n)))
    s = pow2(c)
    if c < n:
        s = s + pow2(2 * c)[0::2][: n - c]
    return np.asarray(s, dtype=np.float32)


def _pos_pieces(pos):
    a64 = (pos >> 6) * 64
    b = pos & 63
    return np.stack([a64, a64, a64, b, b, b]).astype(np.float32)


def _slope_pieces():
    sl = jnp.asarray(_alibi_slopes(MIX_HEADS) * np.float32(LOG2E), F32)
    s1 = sl.astype(BF16).astype(F32)
    s2 = (sl - s1).astype(BF16).astype(F32)
    s3 = (sl - s1 - s2).astype(BF16).astype(F32)
    six = jnp.stack([s1, s2, s3, s1, s2, s3], axis=1)
    return jnp.pad(six, ((0, 0), (0, HEAD_DIM - 6))).reshape(MIX_HEADS, 1, HEAD_DIM)


def _nsa_proj(x, norm1, w_in, gate_b):
    B, S, D = x.shape
    TM = min(512, S)
    G, H, Dh = KV_GROUPS, MIX_HEADS, HEAD_DIM
    scale = HEAD_DIM ** -0.5
    o = np.cumsum([0, MIX_WIDTH] + [KV_WIDTH] * 6 + [3 * MIX_HEADS, MEM_WIDTH])
    wq, wkc, wvc, wks, wvs, wkw, wvw, wgl, wqm = (w_in[:, o[i]:o[i + 1]] for i in range(9))
    wgl = wgl.reshape(D, 3, G, GROUP_SIZE).transpose(0, 2, 1, 3).reshape(D, G, 9)
    wgl = jnp.pad(wgl, ((0, 0), (0, 0), (0, 7))).reshape(D, 64)
    gb = gate_b.reshape(3, G, GROUP_SIZE).transpose(1, 0, 2).reshape(G, 9)
    gb = jnp.pad(gb, ((0, 0), (0, 7))).reshape(1, 64)
    w_row = jnp.concatenate(
        [wq * (scale * LOG2E), wkc, wvc, wvs, wvw, wqm * scale, wgl, jnp.zeros((D, LANES - 64), F32)],
        axis=1).astype(BF16)
    w_t = jnp.concatenate([wks, wkw], axis=1).T.astype(BF16)
    head = lambda n, w: pl.BlockSpec((1, n, TM, w), lambda b, s: (b, 0, s, 0))
    cmpv = pl.BlockSpec((1, G, TM // CMP_STRIDE, CMP_STRIDE * Dh), lambda b, s: (b, 0, s, 0))
    return pl.pallas_call(
        _nsa_proj_kernel,
        grid=(B, S // TM),
        in_specs=[
            pl.BlockSpec((1, TM, D), lambda b, s: (b, s, 0)),
            pl.BlockSpec((1, D), lambda b, s: (0, 0)),
            pl.BlockSpec((D, NSA_ROW_W), lambda b, s: (0, 0)),
            pl.BlockSpec((2 * KV_WIDTH, D), lambda b, s: (0, 0)),
            pl.BlockSpec((1, 64), lambda b, s: (0, 0)),
            pl.BlockSpec((H, 1, Dh), lambda b, s: (0, 0, 0)),
        ],
        out_specs=[
            head(H, LANES), cmpv, cmpv, head(G, LANES), head(G, LANES),
            pl.BlockSpec((1, KV_WIDTH, TM), lambda b, s: (b, 0, s)),
            pl.BlockSpec((1, KV_WIDTH, TM), lambda b, s: (b, 0, s)),
            pl.BlockSpec((1, TM, MEM_WIDTH), lambda b, s: (b, s, 0)),
            pl.BlockSpec((1, G, TM, 16), lambda b, s: (b, 0, s, 0)),
        ],
        out_shape=[
            jax.ShapeDtypeStruct((B, H, S, LANES), BF16),
            jax.ShapeDtypeStruct((B, G, S // CMP_STRIDE, CMP_STRIDE * Dh), BF16),
            jax.ShapeDtypeStruct((B, G, S // CMP_STRIDE, CMP_STRIDE * Dh), BF16),
            jax.ShapeDtypeStruct((B, G, S, LANES), BF16),
            jax.ShapeDtypeStruct((B, G, S, LANES), BF16),
            jax.ShapeDtypeStruct((B, KV_WIDTH, S), BF16),
            jax.ShapeDtypeStruct((B, KV_WIDTH, S), BF16),
            jax.ShapeDtypeStruct((B, S, MEM_WIDTH), BF16),
            jax.ShapeDtypeStruct((B, G, S, 16), F32),
        ],
        scratch_shapes=[pltpu.VMEM((TM, LANES), F32)],
        compiler_params=_cparams(("parallel", "parallel")),
        name="nsa_proj",
    )(x, norm1.reshape(1, D), w_row, w_t, gb, _slope_pieces())


def _compress_kernel(kc_ref, vc_ref, w1_ref, w2_ref, pos_ref, cpos_ref, kcb_ref, vcb_ref):
    half = CMP_STRIDE * HEAD_DIM
    for idx, (src, dst) in enumerate(((kc_ref, kcb_ref), (vc_ref, vcb_ref))):
        c = src[0, 0]
        ncp = c.shape[0]
        a = _mm(c, w1_ref[idx, :half])
        bm = _mm(c, w1_ref[idx, half:])
        bias = _mm(pos_ref[idx], w1_ref[idx])[0:1]
        hid = jax.nn.gelu(a + pltpu.roll(bm, ncp - 1, 0) + bias)
        out = _mm(hid, w2_ref[idx])
        out = jnp.where(_iota(out.shape, 0) < ncp - 1, out, 0.0)
        pad = cpos_ref[...] if idx == 0 else jnp.zeros(out.shape, F32)
        out = jnp.concatenate([out, pad], axis=1)
        dst[0, 0] = out.astype(BF16)


def _compress(kc, vc, cmp_pos, cmp_w1, cmp_w2):
    B, G, NCP, _ = kc.shape
    Dh = HEAD_DIM
    w1 = cmp_w1.reshape(2, CMP_BLOCK * Dh, CMP_HIDDEN).astype(BF16)
    w2 = cmp_w2.astype(BF16)
    pos = jnp.broadcast_to(cmp_pos.reshape(2, 1, CMP_BLOCK * Dh), (2, 8, CMP_BLOCK * Dh)).astype(BF16)
    cend = np.arange(NCP) * CMP_STRIDE + CMP_BLOCK - 1
    cpos = np.zeros((NCP, Dh), np.float32)
    cpos[:, :6] = _pos_pieces(cend).T
    blk = pl.BlockSpec((1, 1, NCP, CMP_STRIDE * Dh), lambda b, g: (b, g, 0, 0))
    return pl.pallas_call(
        _compress_kernel,
        grid=(B, G),
        in_specs=[
            blk, blk,
            pl.BlockSpec((2, CMP_BLOCK * Dh, CMP_HIDDEN), lambda b, g: (0, 0, 0)),
            pl.BlockSpec((2, CMP_HIDDEN, Dh), lambda b, g: (0, 0, 0)),
            pl.BlockSpec((2, 8, CMP_BLOCK * Dh), lambda b, g: (0, 0, 0)),
            pl.BlockSpec((NCP, Dh), lambda b, g: (0, 0)),
        ],
        out_specs=[pl.BlockSpec((1, 1, NCP, LANES), lambda b, g: (b, g, 0, 0))] * 2,
        out_shape=[jax.ShapeDtypeStruct((B, G, NCP, LANES), BF16)] * 2,
        compiler_params=_cparams(("parallel", "parallel")),
        name="nsa_compress",
    )(kc, vc, w1, w2, pos, jnp.asarray(cpos))


def _exp2_bf16(s, m):
    return jnp.concatenate(
        [jnp.exp2((s[:, i * LANES:(i + 1) * LANES] - m).astype(BF16)) for i in range(s.shape[1] // LANES)],
        axis=1)


def _nsa_attn_kernel(q_ref, kcb_ref, vcb_ref, ovt_ref, kst_ref, ksc_ref, vs_ref, kwt_ref, kwc_ref,
                     vw_ref, gate_ref, eg_ref, o_ref, score_scr, m_scr, acc_scr, s_scr, sw_scr, *, TQ, TK, S, n_sel, NG):
    R, Dh, CH = GROUP_SIZE, HEAD_DIM, LANES
    t0 = pl.program_id(2) * TQ
    NCP, NB = S // CMP_STRIDE, S // SEL_BLOCK
    n_sub = TQ // CH
    n_ch = R * n_sub
    groups = list(range(NG))
    chunks = [(g, c) for g in groups for c in range(n_ch)]
    rows_of = lambda c: slice(c * CH, (c + 1) * CH)
    qa = [q_ref[0, g * R:(g + 1) * R].reshape(R * TQ, LANES) for g in groups]
    tq_col = t0 + _iota((TQ, 1), 0)
    tq_col3 = t0 + (_iota((R * TQ, 1), 0) & (TQ - 1))

    sc = [_mm(qa[g], kcb_ref[0, g], _NT) for g in groups]
    WK = WINDOW + TQ
    w0 = pl.multiple_of(jnp.maximum(t0 - WINDOW, 0), LANES)
    consts_w = kwc_ref[:, pl.ds(w0, WK)]
    for g in groups:
        sw_scr[g] = jnp.dot(qa[g], jnp.concatenate(
            [kwt_ref[0, g * Dh:(g + 1) * Dh, pl.ds(w0, WK)], consts_w], axis=0), preferred_element_type=F32)
    n_row = _iota((1, NCP), 1)
    cend = n_row * CMP_STRIDE + (CMP_BLOCK - 1)
    mask_c = cend <= tq_col
    has_c = tq_col >= CMP_BLOCK - 1
    psum = [[jnp.zeros((CH, NCP), F32) for _ in range(n_sub)] for _ in groups]
    o_c = []
    for g, c in chunks:
        h = c % n_sub
        s = jnp.where(mask_c[h * CH:(h + 1) * CH], sc[g][rows_of(c)], NEG_INF)
        p = jnp.exp2(s - jnp.max(s, axis=-1, keepdims=True))
        l = jnp.sum(p, axis=-1, keepdims=True)
        p = p * jnp.where(has_c[h * CH:(h + 1) * CH], 1.0 / l, 0.0)
        psum[g][c % n_sub] = psum[g][c % n_sub] + p
        o_c.append(_mm(p, vcb_ref[0, g]))
    psum = [jnp.concatenate(psum[g], axis=0) for g in groups]
    o_c = [jnp.concatenate(o_c[g * n_ch:(g + 1) * n_ch], axis=0) for g in groups]

    tb = lax.shift_right_logical(t0 + _iota((1, TQ), 1), 6)
    j_col = _iota((NB, 1), 0)
    valid = j_col <= tb
    forced = (j_col == 0) | (j_col == tb) | (j_col == tb - 1)
    n_valid = (t0 + TQ - 1) // SEL_BLOCK + 1
    keys = []
    for g in groups:
        imp_t = _mm(ovt_ref[...], psum[g], _NT, hi=True)
        score = jnp.where(valid, jnp.where(forced, FORCE_SCORE, imp_t), -jnp.inf)
        keys.append(lax.bitcast_convert_type(score, jnp.int32))
        score_scr[g] = keys[g]

    def rank_of(g):
        def rank_body(i, cnt):
            row = score_scr[g, pl.ds(i, 1), :]
            return cnt + jnp.where(row > keys[g] - jnp.where(i < j_col, 1, 0), 1, 0)
        return lax.fori_loop(0, n_valid, rank_body, jnp.zeros((NB, TQ), jnp.int32))

    ranks = [rank_of(g) for g in groups]
    q2 = []
    for g in groups:
        unsel_t = jnp.where(valid, jnp.where(ranks[g] < n_sel, 0.0, 1.0), 1.0)
        unsel = jnp.concatenate([unsel_t, jnp.zeros((LANES - NB, TQ), F32)], axis=0).T.astype(BF16)
        q2.append(jnp.concatenate([qa[g], jnp.concatenate([unsel] * R, axis=0)], axis=1))

    def scores_to(slot, kt):
        k0 = pl.multiple_of(kt * TK, TK)
        consts = ksc_ref[:, pl.ds(k0, TK)]
        for g in groups:
            s_scr[slot, g] = jnp.dot(q2[g], jnp.concatenate(
                [kst_ref[0, g * Dh:(g + 1) * Dh, pl.ds(k0, TK)], consts], axis=0), preferred_element_type=F32)

    scores_to(0, 0)

    kpos_w = w0 + _iota((1, WK), 1)
    mask_w = []
    for h in range(n_sub):
        dist = (t0 + h * CH + _iota((CH, 1), 0)) - kpos_w
        mask_w.append((dist >= 0) & (dist < WINDOW))
    ow = []
    for g, c in chunks:
        s = jnp.where(mask_w[c % n_sub], sw_scr[g, rows_of(c), :], NEG_INF)
        m = jnp.broadcast_to(jnp.max(s, axis=-1, keepdims=True), (CH, LANES))
        ow.append(jnp.dot(_exp2_bf16(s, m), vw_ref[0, g, pl.ds(w0, WK), :], preferred_element_type=F32))
    ow = [jnp.concatenate(ow[g * n_ch:(g + 1) * n_ch], axis=0) for g in groups]

    m_scr[...] = jnp.full(m_scr.shape, NEG_INF, F32)
    acc_scr[...] = jnp.zeros(acc_scr.shape, F32)
    n_kt = (t0 + TQ - 1) // TK + 1

    def accumulate(kt, slot, causal):
        k0 = pl.multiple_of(kt * TK, TK)

        def tile(g, c):
            x = s_scr[slot, g, rows_of(c), :]
            if causal:
                keep = k0 + _iota((1, TK), 1) <= t0 + (c % n_sub) * CH + _iota((CH, 1), 0)
                x = jnp.where(keep, x, NEG_INF)
            return x

        m_old = [m_scr[g, rows_of(c)] for g, c in chunks]
        m_new = [jnp.maximum(mo, jnp.max(tile(g, c), axis=-1, keepdims=True))
                 for mo, (g, c) in zip(m_old, chunks)]
        for i, (g, c) in enumerate(chunks):
            p = _exp2_bf16(tile(g, c), m_new[i])
            pv = jnp.dot(p, vs_ref[0, g, pl.ds(k0, TK), :], preferred_element_type=F32)
            acc_scr[g, rows_of(c)] = jnp.exp2(m_old[i] - m_new[i]) * acc_scr[g, rows_of(c)] + pv
            m_scr[g, rows_of(c)] = m_new[i]

    n_full = n_kt - 1

    def pair_body(j, carry):
        scores_to(1, 2 * j + 1)
        accumulate(2 * j, 0, False)
        scores_to(0, 2 * j + 2)
        accumulate(2 * j + 1, 1, False)
        return carry

    lax.fori_loop(0, n_full // 2, pair_body, 0)

    @pl.when(n_full % 2 == 1)
    def _():
        scores_to(1, n_full)
        accumulate(n_full - 1, 0, False)
        accumulate(n_full, 1, True)

    @pl.when(n_full % 2 == 0)
    def _():
        accumulate(n_full, 0, True)

    for g in groups:
        acc = acc_scr[g]
        o_s = acc * (1.0 / pltpu.roll(acc, Dh, 1))
        o_w = ow[g] * (1.0 / pltpu.roll(ow[g], Dh, 1))
        gv = gate_ref[0, g]
        g_hi = gv.astype(BF16)
        g_lo = (gv - g_hi.astype(F32)).astype(BF16)
        ge = (jnp.dot(g_hi, eg_ref[...], preferred_element_type=F32)
              + jnp.dot(g_lo, eg_ref[...], preferred_element_type=F32))
        gate = lambda k: ge[:, k * LANES:(k + 1) * LANES]
        for r in range(R):
            rows = slice(r * TQ, (r + 1) * TQ)
            out = gate(r) * o_c[g][rows] + gate(R + r) * o_s[rows] + gate(2 * R + r) * o_w[rows]
            o_ref[0, :, (g * R + r) * Dh:(g * R + r + 1) * Dh] = out[:, :Dh].astype(BF16)


def _nsa_attn(q, kcb, vcb, kst, vs, kwt, vw, gates):
    B, H, S, _ = q.shape
    G, R, Dh = KV_GROUPS, GROUP_SIZE, HEAD_DIM
    TQ = 256
    TK = min(512, S)
    NG = 2
    NCP, NB = S // CMP_STRIDE, S // SEL_BLOCK
    n_sel = min(SEL_TOPN, NB)
    assert S % TK == 0 and S >= WINDOW + TQ and NB <= Dh and S <= 4096
    cs = np.arange(NCP) * CMP_STRIDE
    ss = np.arange(NB) * SEL_BLOCK
    ov = (cs[:, None] <= ss[None, :] + SEL_BLOCK - 1) & (cs[:, None] + CMP_BLOCK - 1 >= ss[None, :])
    ov[NCP - 1] = False
    ovt = jnp.asarray(ov.T.astype(np.float32))
    pieces = _pos_pieces(np.arange(S))
    ksc = np.zeros((3 * Dh, S), np.float32)
    ksc[:6] = pieces
    ksc[Dh:Dh + NB] = np.where(np.arange(S)[None, :] // SEL_BLOCK == np.arange(NB)[:, None], NEG_INF, 0.0)
    kwc = np.zeros((Dh, S), np.float32)
    kwc[:6] = pieces
    eg = (np.arange(9 * LANES)[None, :] // LANES == np.arange(16)[:, None]).astype(np.float32)
    kern = functools.partial(_nsa_attn_kernel, TQ=TQ, TK=TK, S=S, n_sel=n_sel, NG=NG)
    return pl.pallas_call(
        kern,
        grid=(B, G // NG, S // TQ),
        in_specs=[
            pl.BlockSpec((1, NG * R, TQ, LANES), lambda b, g, i: (b, g, i, 0)),
            pl.BlockSpec((1, NG, NCP, LANES), lambda b, g, i: (b, g, 0, 0)),
            pl.BlockSpec((1, NG, NCP, LANES), lambda b, g, i: (b, g, 0, 0)),
            pl.BlockSpec((NB, NCP), lambda b, g, i: (0, 0)),
            pl.BlockSpec((1, NG * Dh, S), lambda b, g, i: (b, g, 0)),
            pl.BlockSpec((3 * Dh, S), lambda b, g, i: (0, 0)),
            pl.BlockSpec((1, NG, S, LANES), lambda b, g, i: (b, g, 0, 0)),
            pl.BlockSpec((1, NG * Dh, S), lambda b, g, i: (b, g, 0)),
            pl.BlockSpec((Dh, S), lambda b, g, i: (0, 0)),
            pl.BlockSpec((1, NG, S, LANES), lambda b, g, i: (b, g, 0, 0)),
            pl.BlockSpec((1, NG, TQ, 16), lambda b, g, i: (b, g, i, 0)),
            pl.BlockSpec((16, 9 * LANES), lambda b, g, i: (0, 0)),
        ],
        out_specs=pl.BlockSpec((1, TQ, NG * R * Dh), lambda b, g, i: (b, i, g)),
        out_shape=jax.ShapeDtypeStruct((B, S, G * R * Dh), BF16),
        scratch_shapes=[
            pltpu.VMEM((NG, NB, TQ), jnp.int32),
            pltpu.VMEM((NG, R * TQ, LANES), F32),
            pltpu.VMEM((NG, R * TQ, LANES), F32),
            pltpu.VMEM((2, NG, R * TQ, TK), F32),
            pltpu.VMEM((NG, R * TQ, WINDOW + TQ), F32),
        ],
        compiler_params=_cparams(("parallel", "parallel", "arbitrary")),
        name="nsa_attn",
    )(q, kcb, vcb, ovt, kst, jnp.asarray(ksc, BF16), vs, kwt, jnp.asarray(kwc, BF16), vw, gates,
      jnp.asarray(eg, BF16))


RW_Z_W = 3 * MIX_WIDTH + 2 * LANES + 2 * LANES
RW_ROW_W = RW_Z_W + MEM_WIDTH


def _rw_prep_kernel(x_ref, g_ref, w_ref, mu_ref, w0_ref, w2_ref, a0_ref, a2_ref, g2_ref, kk_ref,
                    ka_ref, r_ref, lw_ref, kx_ref, km_ref, v_ref, a_ref, go_ref, qm_ref, carry_scr):
    W = MIX_WIDTH

    @pl.when(pl.program_id(1) == 0)
    def _():
        carry_scr[...] = jnp.zeros(carry_scr.shape, F32)

    hn = _rms(x_ref[0], g_ref[...]).astype(BF16)
    res = _mm(hn, w_ref[...])
    qm_ref[0] = res[:, RW_Z_W:].astype(BF16)
    z = res[:, :RW_Z_W]
    tm = z.shape[0]
    zprev = jnp.where(_iota((tm, 1), 0) == 0, carry_scr[0:1, :], pltpu.roll(z, 1, 0))
    carry_scr[0:1, :] = z[tm - 1:tm, :]
    z = z + (zprev - z) * mu_ref[...]
    r, k, v = z[:, :W], z[:, W:2 * W], z[:, 2 * W:3 * W]
    zw = z[:, 3 * W:3 * W + LANES]
    za = z[:, 3 * W + LANES:3 * W + 2 * LANES]
    zg = z[:, 3 * W + 2 * LANES:]
    w_log = -jax.nn.softplus(-(w0_ref[...] + _mm(jnp.tanh(zw), w2_ref[...]))) - 0.5
    a = jax.nn.sigmoid(a0_ref[...] + _mm(za, a2_ref[...]))
    r_ref[0] = r
    lw_ref[0] = -jnp.exp(w_log)
    kx_ref[0] = k * kk_ref[...]
    km_ref[0] = k * (1.0 + (a - 1.0) * ka_ref[...])
    v_ref[0] = v
    a_ref[0] = a
    go_ref[0] = _mm(jax.nn.sigmoid(zg), g2_ref[...])


def _pad_rows(w, n):
    return jnp.pad(w, ((0, n - w.shape[0]), (0, 0)))


def _rw_prep(x, norm1, w_in, mu, w0, w2, a0, a2, g2, k_k, k_a):
    B, S, D = x.shape
    W = MIX_WIDTH
    TM = min(256, S)
    o = np.cumsum([0, 3 * W, DECAY_LORA, ICLR_LORA, GATE_LORA, MEM_WIDTH])
    seg = [w_in[:, o[i]:o[i + 1]] for i in range(5)]
    padc = lambda w, n: jnp.pad(w, ((0, 0), (0, n - w.shape[1])))
    w_row = jnp.concatenate(
        [seg[0], padc(seg[1], LANES), padc(seg[2], LANES), padc(seg[3], 2 * LANES),
         seg[4] * HEAD_DIM ** -0.5], axis=1).astype(BF16)
    mus = [mu[o[i]:o[i + 1]] for i in range(4)]
    padv = lambda v, n: jnp.pad(v, (0, n - v.shape[0]))
    mu_p = jnp.concatenate([mus[0], padv(mus[1], LANES), padv(mus[2], LANES),
                            padv(mus[3], 2 * LANES)]).reshape(1, RW_Z_W)
    vec = lambda v: v.reshape(1, W)
    full = lambda a: pl.BlockSpec(a.shape, lambda b, s: (0,) * a.ndim)
    args = [norm1.reshape(1, D), w_row, mu_p, vec(w0), _pad_rows(w2, LANES).astype(BF16), vec(a0),
            _pad_rows(a2, LANES).astype(BF16), _pad_rows(g2, 2 * LANES).astype(BF16), vec(k_k), vec(k_a)]
    oblk = pl.BlockSpec((1, TM, W), lambda b, s: (b, s, 0))
    return pl.pallas_call(
        _rw_prep_kernel,
        grid=(B, S // TM),
        in_specs=[pl.BlockSpec((1, TM, D), lambda b, s: (b, s, 0))] + [full(a) for a in args],
        out_specs=[oblk] * 7 + [pl.BlockSpec((1, TM, MEM_WIDTH), lambda b, s: (b, s, 0))],
        out_shape=[jax.ShapeDtypeStruct((B, S, W), F32)] * 7
        + [jax.ShapeDtypeStruct((B, S, MEM_WIDTH), BF16)],
        scratch_shapes=[pltpu.VMEM((8, RW_Z_W), F32)],
        compiler_params=_cparams(("parallel", "arbitrary")),
        name="rw_prep",
    )(x, *args)


def _split2(x):
    hi = x.astype(BF16)
    return hi, (x - hi.astype(F32)).astype(BF16)


def _mm3(a, b):
    dot = functools.partial(jnp.dot, preferred_element_type=F32)
    return dot(a[0], b[0]) + dot(a[0], b[1]) + dot(a[1], b[0])


def _tri_inverse(a2s, eye, same16, same32, same64):
    ds = _each(lambda a2: jnp.where(same16, a2, 0.0), a2s)
    xs = _each(lambda d: eye + d, ds)
    for _ in range(3):
        ds = _each(lambda d: _mm(d, d), ds)
        xs = _each(lambda x, d: x + _mm(x, d), xs, ds)
    for lo, hi_ in ((same16, same32), (same32, same64)):
        sel = hi_ & jnp.logical_not(lo)
        mids = _each(lambda x, a2: _mm(x, jnp.where(sel, a2, 0.0)), xs, a2s)
        xs = _each(lambda x, m: x + _mm(m, x), xs, mids)
    res = _each(lambda x, a2: (eye - x) + _mm3(_split2(a2), _split2(x)), xs, a2s)
    return _each(lambda x, r: x + _mm(x, r), xs, res)


def _rw_scan_kernel(r_ref, lw_ref, kx_ref, km_ref, v_ref, a_ref, g_ref, rk_ref, lnw_ref, lnb_ref,
                    o_ref, st_scr, *, TS, C):
    Dh = HEAD_DIM

    @pl.when(pl.program_id(1) == 0)
    def _():
        st_scr[...] = jnp.zeros(st_scr.shape, F32)

    head0 = _iota((1, LANES), 1) < Dh
    ltri = jnp.where(_iota((C, C), 1) <= _iota((C, C), 0), 1.0, 0.0)
    col2 = _iota((C, LANES), 1) & (Dh - 1)
    row2 = _iota((C, LANES), 0)
    m_incl = col2 <= row2
    m_strict = col2 < row2
    r128 = _iota((LANES, LANES), 0)
    c128 = _iota((LANES, LANES), 1)
    eye = jnp.where(r128 == c128, 1.0, 0.0)
    same16 = (r128 >> 4) == (c128 >> 4)
    same32 = (r128 >> 5) == (c128 >> 5)
    same64 = (r128 >> 6) == (c128 >> 6)
    zeros = jnp.zeros((C, LANES), F32)
    ltri = ltri.astype(BF16)

    def hsum(x):
        s0 = jnp.sum(jnp.where(head0, x, 0.0), axis=-1, keepdims=True)
        s1 = jnp.sum(jnp.where(head0, 0.0, x), axis=-1, keepdims=True)
        return jnp.where(head0, s0, s1)

    n_pairs = MIX_WIDTH // LANES
    N_AHEAD = 2
    lns = [slice(pi * LANES, (pi + 1) * LANES) for pi in range(n_pairs)]
    pis = list(range(n_pairs))

    def cumsum_decay(lw):
        l1 = lw.astype(BF16)
        l2 = (lw - l1.astype(F32)).astype(BF16)
        l3 = (lw - l1.astype(F32) - l2.astype(F32)).astype(BF16)
        cum3 = jnp.dot(ltri, jnp.concatenate([l1, l2, l3], axis=1), preferred_element_type=F32)
        return cum3[:, :LANES] + cum3[:, LANES:2 * LANES] + cum3[:, 2 * LANES:]

    def scaled(ln, cum, sl):
        r, lw, kx, km, a = r_ref[0, sl, ln], lw_ref[0, sl, ln], kx_ref[0, sl, ln], km_ref[0, sl, ln], a_ref[0, sl, ln]
        kk = kx / jnp.maximum(jnp.sqrt(hsum(kx * kx)), 1e-12)
        p_in = jnp.exp(cum)
        at = -kk * jnp.exp(cum - lw)
        rt = r * p_in
        p_inv = jnp.exp(-cum)
        bk = jnp.concatenate([kk * a * p_inv, km * p_inv], axis=0)
        at0, at1 = jnp.where(head0, at, 0.0), jnp.where(head0, 0.0, at)
        rt0, rt1 = jnp.where(head0, rt, 0.0), jnp.where(head0, 0.0, rt)
        lhs = jnp.concatenate([at0, at1, rt0, rt1], axis=0)
        return lhs, bk, rt, p_in[C - 1:C, :]

    def split_aa(aa):
        aa0 = jnp.where(m_strict, aa[0:C], 0.0)
        aa1 = pltpu.roll(jnp.where(m_strict, aa[C:2 * C], 0.0), Dh, 1)
        ar0 = jnp.where(m_incl, aa[2 * C:3 * C], 0.0)
        ar1 = jnp.where(m_incl, aa[3 * C:4 * C], 0.0)
        a2 = jnp.concatenate([jnp.where(head0, aa0, 0.0), jnp.where(head0, 0.0, aa1)], axis=0)
        return aa0, aa1, ar0, ar1, a2

    def epilogue(ln, y, sl):
        r, km, v = r_ref[0, sl, ln], km_ref[0, sl, ln], v_ref[0, sl, ln]
        mean = hsum(y) * (1.0 / Dh)
        d = y - mean
        var = hsum(d * d) * (1.0 / Dh)
        yn = d * lax.rsqrt(var + GN_EPS) * lnw_ref[:, ln] + lnb_ref[:, ln]
        bonus = hsum(r * km * rk_ref[:, ln]) * v
        o_ref[0, sl, ln] = ((yn + bonus) * g_ref[0, sl, ln]).astype(BF16)

    def chunk_group(cg, carry):
        sls = [pl.ds(pl.multiple_of((cg * N_AHEAD + j) * C, C), C) for j in range(N_AHEAD)]
        sl_i = [sl for sl in sls for _ in lns]
        ln_i = [ln for _ in sls for ln in lns]
        cums = _each(lambda sl, ln: cumsum_decay(lw_ref[0, sl, ln]), sl_i, ln_i)
        lhss, bks, rts, pcs = zip(*_each(scaled, ln_i, cums, sl_i))
        aas = _each(lambda lhs, bk: _mm(lhs, bk, _NT), lhss, bks)
        aa0s, aa1s, ar0s, ar1s, a2s = zip(*_each(split_aa, aas))
        t2s = _tri_inverse(a2s, eye, same16, same32, same64)
        vs = _each(lambda sl, ln: v_ref[0, sl, ln], sl_i, ln_i)
        x0s = _each(lambda aa0, v: _mm(aa0, jnp.concatenate([zeros, v], axis=0)), aa0s, vs)
        x1s = _each(lambda aa1, v: _mm(aa1, jnp.concatenate([v, zeros], axis=0)), aa1s, vs)
        wus = _each(lambda t2, lhs, x0, x1: _mm(t2, jnp.concatenate(
            [lhs[:2 * C], jnp.concatenate([x0, x1], axis=0)], axis=1)), t2s, lhss, x0s, x1s)
        for j, sl in enumerate(sls):
            k = slice(j * n_pairs, (j + 1) * n_pairs)
            sts = _each(lambda pi: st_scr[pi], pis)
            us = _each(lambda wu, st: _mm(wu[:C, :LANES] + wu[C:, :LANES], st, _NT)
                       + jnp.where(head0, wu[:C, LANES:], wu[C:, LANES:]), wus[k], sts)
            uvs = _each(lambda u, v: jnp.concatenate([u, v], axis=0), us, vs[k])
            ys = _each(lambda rt, st, ar0, ar1, uv: _mm(rt, st, _NT)
                       + jnp.where(head0, _mm(ar0, uv), _mm(ar1, uv)), rts[k], sts, ar0s[k], ar1s[k], uvs)
            new = _each(lambda st, pc, uv, bk: st * pc + jnp.where(same64, _mm(uv, bk * pc, _TN), 0.0),
                        sts, pcs[k], uvs, bks[k])
            for pi in pis:
                st_scr[pi] = new[pi]
            _each(lambda ln, y: epilogue(ln, y, sl), lns, ys)
        return carry

    lax.fori_loop(0, TS // (C * N_AHEAD), chunk_group, 0)


def _rw_scan(r, lw, kx, km, v, a, g, r_k, lnx_w, lnx_b):
    B, S, W = r.shape
    TS = min(256, S)
    C = 64
    blk = pl.BlockSpec((1, TS, W), lambda b, s: (b, s, 0))
    vblk = pl.BlockSpec((1, W), lambda b, s: (0, 0))
    kern = functools.partial(_rw_scan_kernel, TS=TS, C=C)
    return pl.pallas_call(
        kern,
        grid=(B, S // TS),
        in_specs=[blk] * 7 + [vblk] * 3,
        out_specs=blk,
        out_shape=jax.ShapeDtypeStruct((B, S, W), BF16),
        scratch_shapes=[pltpu.VMEM((W // LANES, LANES, LANES), F32)],
        compiler_params=_cparams(("parallel", "arbitrary")),
        name="rw_scan",
    )(r, lw, kx, km, v, a, g, r_k.reshape(1, W), lnx_w.reshape(1, W), lnx_b.reshape(1, W))


def kernel(x, mem, norm1, norm_mem, w_mem_kv, w_o, norm2, w_ffn_in, w_ffn_out, nsa_w_in, nsa_gate_b,
           nsa_cmp_pos, nsa_cmp_w1, nsa_cmp_w2, rw_w_in, rw_mu, rw_w0, rw_w2, rw_a0, rw_a2, rw_g2,
           rw_k_k, rw_k_a, rw_r_k, rw_lnx_w, rw_lnx_b, final_norm):
    depth = norm1.shape[0]
    B, S, _ = x.shape
    ktm, vm = _mem_kv(mem, norm_mem, w_mem_kv)
    for i in range(depth):
        j = i // 2
        if i % 2 == 0:
            q, kc, vc, vs, vw, kst, kwt, qm, gates = _nsa_proj(x, norm1[i], nsa_w_in[j], nsa_gate_b[j])
            kcb, vcb = _compress(kc, vc, nsa_cmp_pos[j], nsa_cmp_w1[j], nsa_cmp_w2[j])
            mix = _nsa_attn(q, kcb, vcb, kst, vs, kwt, vw, gates)
        else:
            r, lw, kx, km, v, a, g, qm = _rw_prep(x, norm1[i], rw_w_in[j], rw_mu[j], rw_w0[j], rw_w2[j],
                                                  rw_a0[j], rw_a2[j], rw_g2[j], rw_k_k[j], rw_k_a[j])
            mix = _rw_scan(r, lw, kx, km, v, a, g, rw_r_k[j], rw_lnx_w[j], rw_lnx_b[j])
        x = _post(x, mix, qm, ktm[i], vm[i], w_o[i], norm2[i], w_ffn_in[i], w_ffn_out[i], final_norm,
                  final=(i == depth - 1))
    return x
```
